```python
import jax, jax.numpy as jnp
from jax import lax
import numpy as np

D_MODEL = 4096
BATCH = 4
SEQ = 4096
DEPTH = 1

ATTN_HEAD_DIM = 128
ATTN_HEADS = D_MODEL // 256
ATTN_WIDTH = ATTN_HEADS * ATTN_HEAD_DIM
ROT_DIM = ATTN_HEAD_DIM // 4
ROPE_THETA = 500000.0
MOBA_BLOCK = 256
MOBA_TOPK = 3
ATTN_Q_BLOCK = 16

SSD_D_INNER = D_MODEL
SSD_HEAD_DIM = 64
SSD_HEADS = SSD_D_INNER // SSD_HEAD_DIM
SSD_GROUPS = 8
SSD_HEADS_PER_GROUP = SSD_HEADS // SSD_GROUPS
SSD_STATE = 128
SSD_CONV = 4
SSD_CHUNK = 256
SSD_CONV_DIM = SSD_D_INNER + 2 * SSD_GROUPS * SSD_STATE

N_BRANCHES = 2
IN_COLS = 4 * ATTN_WIDTH + SSD_D_INNER + SSD_CONV_DIM + SSD_HEADS + N_BRANCHES * D_MODEL
IN_SPLITS = [ATTN_WIDTH, 2 * ATTN_WIDTH, 3 * ATTN_WIDTH, 4 * ATTN_WIDTH,
             4 * ATTN_WIDTH + SSD_D_INNER,
             4 * ATTN_WIDTH + SSD_D_INNER + SSD_CONV_DIM,
             4 * ATTN_WIDTH + SSD_D_INNER + SSD_CONV_DIM + SSD_HEADS]
NORM_EPS = 1e-6

kernel_name = "moba_ssd_gated_hybrid"


def rms_norm(x, w):
    xf = x.astype(jnp.float32)
    y = xf * lax.rsqrt(jnp.mean(xf * xf, axis=-1, keepdims=True) + NORM_EPS)
    return (y * w.astype(jnp.float32)).astype(x.dtype)


def partial_rotary(t, pos):
    half = ROT_DIM // 2
    inv_freq = ROPE_THETA ** (-jnp.arange(half, dtype=jnp.float32) * 2.0 / ROT_DIM)
    ang = pos.astype(jnp.float32)[:, None] * inv_freq[None, :]
    cos, sin = jnp.cos(ang), jnp.sin(ang)
    tf = t.astype(jnp.float32)
    t1, t2 = tf[..., :half], tf[..., half:ROT_DIM]
    out = jnp.concatenate([t1 * cos - t2 * sin, t2 * cos + t1 * sin, tf[..., ROT_DIM:]], axis=-1)
    return out.astype(t.dtype)


def moba_attention(q, k, v):
    bsz, n_heads, seq, hd = q.shape
    n_blocks = -(-seq // MOBA_BLOCK)
    s_pad = n_blocks * MOBA_BLOCK
    pad = [(0, 0), (0, 0), (0, s_pad - seq), (0, 0)]
    q, k, v = jnp.pad(q, pad), jnp.pad(k, pad), jnp.pad(v, pad)
    k_blocks = k.reshape(bsz, n_heads, n_blocks, MOBA_BLOCK, hd)
    v_blocks = v.reshape(bsz, n_heads, n_blocks, MOBA_BLOCK, hd)
    k_mean = jnp.mean(k_blocks.astype(jnp.float32), axis=3)
    n_sel = min(MOBA_TOPK, n_blocks)
    scale = hd ** -0.5
    b_idx = jnp.arange(bsz)[:, None, None, None]
    h_idx = jnp.arange(n_heads)[None, :, None, None]

    def query_block(start):
        qc = lax.dynamic_slice_in_dim(q, start, ATTN_Q_BLOCK, axis=2)
        own = start // MOBA_BLOCK
        q_pos = start + jnp.arange(ATTN_Q_BLOCK)
        gate = jnp.einsum('bhqd,bhnd->bhqn', qc.astype(jnp.float32), k_mean)
        gate = jnp.where(jnp.arange(n_blocks) < own, gate, -jnp.inf)
        _, sel = lax.top_k(gate, n_sel)
        sel_valid = jnp.arange(n_sel) < own
        k_sel = k_blocks[b_idx, h_idx, sel]
        v_sel = v_blocks[b_idx, h_idx, sel]
        s_sel = jnp.einsum('bhqd,bhqkjd->bhqkj', qc, k_sel).astype(jnp.float32) * scale
        s_sel = jnp.where(sel_valid[:, None], s_sel, -jnp.inf)
        k_own = lax.dynamic_slice_in_dim(k, own * MOBA_BLOCK, MOBA_BLOCK, axis=2)
        v_own = lax.dynamic_slice_in_dim(v, own * MOBA_BLOCK, MOBA_BLOCK, axis=2)
        s_own = jnp.einsum('bhqd,bhjd->bhqj', qc, k_own).astype(jnp.float32) * scale
        k_pos = own * MOBA_BLOCK + jnp.arange(MOBA_BLOCK)
        s_own = jnp.where(k_pos[None, :] <= q_pos[:, None], s_own, -jnp.inf)
        scores = jnp.concatenate(
            [s_sel.reshape(bsz, n_heads, ATTN_Q_BLOCK, n_sel * MOBA_BLOCK), s_own], axis=-1)
        p = jax.nn.softmax(scores, axis=-1)
        p_sel = p[..., :n_sel * MOBA_BLOCK].reshape(bsz, n_heads, ATTN_Q_BLOCK, n_sel, MOBA_BLOCK)
        p_own = p[..., n_sel * MOBA_BLOCK:]
        o = (jnp.einsum('bhqkj,bhqkjd->bhqd', p_sel.astype(v.dtype), v_sel)
             + jnp.einsum('bhqj,bhjd->bhqd', p_own.astype(v.dtype), v_own))
        return o.astype(q.dtype)

    starts = jnp.arange(0, s_pad, ATTN_Q_BLOCK)
    out = lax.map(query_block, starts)
    out = jnp.moveaxis(out, 0, 2).reshape(bsz, n_heads, s_pad, hd)
    return out[:, :, :seq]


def ssd_chunked_scan(xdt, a, b_mat, c_mat):
    bsz, seq = xdt.shape[:2]
    n_chunks = -(-seq // SSD_CHUNK)
    pad = n_chunks * SSD_CHUNK - seq

    def to_chunks(t):
        t = jnp.pad(t, [(0, 0), (0, pad)] + [(0, 0)] * (t.ndim - 2))
        t = t.reshape(bsz, n_chunks, SSD_CHUNK, *t.shape[2:])
        return jnp.moveaxis(t, 1, 0)

    xs = to_chunks(xdt.reshape(bsz, seq, SSD_GROUPS, SSD_HEADS_PER_GROUP, SSD_HEAD_DIM))
    as_ = to_chunks(a.reshape(bsz, seq, SSD_GROUPS, SSD_HEADS_PER_GROUP))
    bs = to_chunks(b_mat)
    cs = to_chunks(c_mat)
    causal = jnp.tril(jnp.ones((SSD_CHUNK, SSD_CHUNK), dtype=bool))

    def step(state, inp):
        xc, ac, bc, cc = inp
        acum = jnp.cumsum(ac, axis=1)
        seg = acum[:, :, None] - acum[:, None, :]
        decay = jnp.exp(jnp.where(causal[None, :, :, None, None], seg, -jnp.inf))
        cb = jnp.einsum('blgn,bsgn->blsg', cc, bc)
        y_diag = jnp.einsum('blsg,blsgh,bsghp->blghp', cb, decay, xc)
        y_off = jnp.einsum('blgn,bghpn->blghp', cc, state) * jnp.exp(acum)[..., None]
        a_tot = acum[:, -1]
        w = jnp.exp(a_tot[:, None] - acum)
        new_state = (state * jnp.exp(a_tot)[..., None, None]
                     + jnp.einsum('bsgn,bsgh,bsghp->bghpn', bc, w, xc))
        return new_state, y_diag + y_off

    state0 = jnp.zeros((bsz, SSD_GROUPS, SSD_HEADS_PER_GROUP, SSD_HEAD_DIM, SSD_STATE), jnp.float32)
    _, ys = lax.scan(step, state0, (xs, as_, bs, cs))
    ys = jnp.moveaxis(ys, 0, 1).reshape(bsz, n_chunks * SSD_CHUNK, SSD_HEADS, SSD_HEAD_DIM)
    return ys[:, :seq]


def ssd_branch(z, xbc, dt_raw, conv_w, conv_b, dt_bias, a_log, d_skip, norm_w):
    bsz, seq, _ = xbc.shape
    xbc = lax.conv_general_dilated(
        xbc, conv_w[:, None, :], window_strides=(1,), padding=[(SSD_CONV - 1, 0)],
        dimension_numbers=('NWC', 'WIO', 'NWC'), feature_group_count=SSD_CONV_DIM)
    xbc = jax.nn.silu((xbc + conv_b).astype(jnp.float32))
    xs, b_mat, c_mat = jnp.split(xbc, [SSD_D_INNER, SSD_D_INNER + SSD_GROUPS * SSD_STATE], axis=-1)
    xs = xs.reshape(bsz, seq, SSD_HEADS, SSD_HEAD_DIM)
    b_mat = b_mat.reshape(bsz, seq, SSD_GROUPS, SSD_STATE)
    c_mat = c_mat.reshape(bsz, seq, SSD_GROUPS, SSD_STATE)
    dt = jax.nn.softplus(dt_raw.astype(jnp.float32) + dt_bias.astype(jnp.float32))
    a = dt * (-jnp.exp(a_log.astype(jnp.float32)))
    y = ssd_chunked_scan(xs * dt[..., None], a, b_mat, c_mat) + d_skip.astype(jnp.float32)[:, None] * xs
    y = y.reshape(bsz, seq, SSD_D_INNER) * jax.nn.silu(z.astype(jnp.float32))
    yg = y.reshape(bsz, seq, SSD_GROUPS, SSD_D_INNER // SSD_GROUPS)
    yg = yg * lax.rsqrt(jnp.mean(yg * yg, axis=-1, keepdims=True) + NORM_EPS)
    return (yg.reshape(bsz, seq, SSD_D_INNER) * norm_w.astype(jnp.float32)).astype(z.dtype)


def setup_inputs(seed: int = 0) -> dict:
    key = jax.random.key(seed)
    ks = jax.random.split(key, 16)
    nrm = jax.random.normal
    f32 = jnp.float32
    x = nrm(ks[0], (BATCH, SEQ, D_MODEL), f32)
    norm_w = 1.0 + 0.02 * nrm(ks[1], (DEPTH, D_MODEL), f32)
    w_in = nrm(ks[2], (DEPTH, D_MODEL, IN_COLS), f32) * D_MODEL ** -0.5
    conv_w = nrm(ks[3], (DEPTH, SSD_CONV, SSD_CONV_DIM), f32) * SSD_CONV ** -0.5
    conv_b = 0.02 * nrm(ks[4], (DEPTH, SSD_CONV_DIM), f32)
    u = jax.random.uniform(ks[5], (DEPTH, SSD_HEADS), f32)
    dt0 = jnp.exp(u * (jnp.log(0.1) - jnp.log(0.001)) + jnp.log(0.001))
    dt_bias = dt0 + jnp.log(-jnp.expm1(-dt0))
    a_log = jnp.log(jax.random.uniform(ks[6], (DEPTH, SSD_HEADS), f32, 1.0, 16.0))
    d_skip = 1.0 + 0.1 * nrm(ks[7], (DEPTH, SSD_HEADS), f32)
    ssd_norm_w = 1.0 + 0.02 * nrm(ks[8], (DEPTH, SSD_D_INNER), f32)
    w_attn_out = nrm(ks[9], (DEPTH, ATTN_WIDTH, D_MODEL), f32) * ATTN_WIDTH ** -0.5
    w_ssd_out = nrm(ks[10], (DEPTH, SSD_D_INNER, D_MODEL), f32) * SSD_D_INNER ** -0.5
    gate_bias = 0.02 * nrm(ks[11], (DEPTH, N_BRANCHES * D_MODEL), f32)
    w_out = nrm(ks[12], (DEPTH, D_MODEL, D_MODEL), f32) * D_MODEL ** -0.5
    final_norm_w = 1.0 + 0.02 * nrm(ks[13], (D_MODEL,), f32)
    return {"x": x, "norm_w": norm_w, "w_in": w_in, "conv_w": conv_w, "conv_b": conv_b,
            "dt_bias": dt_bias, "a_log": a_log, "d_skip": d_skip, "ssd_norm_w": ssd_norm_w,
            "w_attn_out": w_attn_out, "w_ssd_out": w_ssd_out, "gate_bias": gate_bias,
            "w_out": w_out, "final_norm_w": final_norm_w}


def reference(x, norm_w, w_in, conv_w, conv_b, dt_bias, a_log, d_skip, ssd_norm_w,
              w_attn_out, w_ssd_out, gate_bias, w_out, final_norm_w):
    bsz, seq, _ = x.shape
    pos = jnp.arange(seq)
    h = x
    for l in range(DEPTH):
        u = rms_norm(h, norm_w[l])
        proj = jnp.einsum('bsd,dc->bsc', u, w_in[l])
        q, k, v, g_attn, z, xbc, dt_raw, g_merge = jnp.split(proj, IN_SPLITS, axis=-1)

        def to_heads(t):
            return t.reshape(bsz, seq, ATTN_HEADS, ATTN_HEAD_DIM).transpose(0, 2, 1, 3)
        qh = partial_rotary(to_heads(q), pos)
        kh = partial_rotary(to_heads(k), pos)
        o = moba_attention(qh, kh, to_heads(v))
        o = o.transpose(0, 2, 1, 3).reshape(bsz, seq, ATTN_WIDTH)
        o = (o.astype(jnp.float32) * jax.nn.silu(g_attn.astype(jnp.float32))).astype(x.dtype)
        y_attn = jnp.einsum('bsc,cd->bsd', o, w_attn_out[l])

        y_ssd = ssd_branch(z, xbc, dt_raw, conv_w[l], conv_b[l], dt_bias[l], a_log[l],
                           d_skip[l], ssd_norm_w[l])
        y_ssd = jnp.einsum('bsc,cd->bsd', y_ssd, w_ssd_out[l])

        gm = jax.nn.sigmoid((g_merge + gate_bias[l]).astype(jnp.float32))
        merged = (gm[..., :D_MODEL] * y_attn.astype(jnp.float32)
                  + gm[..., D_MODEL:] * y_ssd.astype(jnp.float32)).astype(x.dtype)
        h = h + jnp.einsum('bsd,de->bse', merged, w_out[l])
    return rms_norm(h, final_norm_w)
```

```python
import functools

import jax
import jax.numpy as jnp
from jax import lax
from jax.experimental import pallas as pl
from jax.experimental.pallas import tpu as pltpu

F32 = jnp.float32
BF16 = jnp.bfloat16

NORM_EPS = 1e-6
ATTN_HEAD_DIM = 128
ROT_DIM = ATTN_HEAD_DIM // 4
ROPE_THETA = 500000.0
MOBA_BLOCK = 256
MOBA_TOPK = 3
SSD_HEAD_DIM = 64
SSD_STATE = 128
SSD_CONV = 4
SSD_CHUNK = 256
CONV_CARRY_ROWS = 8

V7X_VMEM_LIMIT_BYTES = 56 * 1024 * 1024
LANES = 128

NT_DIMS = (((1,), (1,)), ((), ()))
TN_DIMS = (((0,), (0,)), ((), ()))


def _params(*semantics):
    return pltpu.CompilerParams(dimension_semantics=semantics,
                                vmem_limit_bytes=V7X_VMEM_LIMIT_BYTES)


def _tile(n, pref):
    for t in range(min(n, pref), 0, -LANES):
        if n % t == 0:
            return t
    raise ValueError(f"no lane-aligned tile for {n}")


def _rmsnorm_kernel(x_ref, w_ref, o_ref):
    x = x_ref[...]
    ms = jnp.mean(x * x, axis=-1, keepdims=True)
    o_ref[...] = (x * lax.rsqrt(ms + NORM_EPS) * w_ref[...]).astype(o_ref.dtype)


def _rmsnorm(x, w, out_dtype):
    t, d = x.shape
    tm = _tile(t, 256)
    return pl.pallas_call(
        _rmsnorm_kernel,
        grid=(t // tm,),
        in_specs=[pl.BlockSpec((tm, d), lambda i: (i, 0)),
                  pl.BlockSpec((1, d), lambda i: (0, 0))],
        out_specs=pl.BlockSpec((tm, d), lambda i: (i, 0)),
        out_shape=jax.ShapeDtypeStruct((t, d), out_dtype),
        compiler_params=_params("parallel"),
        name="rmsnorm",
    )(x, w.reshape(1, d))


def _matmul_kernel(x_ref, w_ref, o_ref):
    o_ref[...] = jnp.dot(x_ref[...], w_ref[...], preferred_element_type=F32).astype(o_ref.dtype)


def _matmul(x, w, out_dtype, name):
    m, k = x.shape
    _, n = w.shape
    tm, tn = _tile(m, 1024), _tile(n, 1024)
    return pl.pallas_call(
        _matmul_kernel,
        grid=(m // tm, n // tn),
        in_specs=[pl.BlockSpec((tm, k), lambda i, j: (i, 0)),
                  pl.BlockSpec((k, tn), lambda i, j: (0, j))],
        out_specs=pl.BlockSpec((tm, tn), lambda i, j: (i, j)),
        out_shape=jax.ShapeDtypeStruct((m, n), out_dtype),
        compiler_params=_params("parallel", "arbitrary"),
        name=name,
    )(x, w)


def _rotary_kernel(qk_ref, cos_ref, sin_lo_ref, sin_hi_ref, o_ref, kmean_ref, *, n_heads, q_scale):
    cos, sin_lo, sin_hi = cos_ref[...], sin_lo_ref[...], sin_hi_ref[...]
    half = ROT_DIM // 2

    def rot(col):
        t = qk_ref[:, pl.ds(col, LANES)].astype(F32)
        return (t * cos + pltpu.roll(t, LANES - half, 1) * sin_lo + pltpu.roll(t, half, 1) * sin_hi)

    def head(h, carry):
        qcol = pl.multiple_of(h * LANES, LANES)
        kcol = pl.multiple_of((n_heads + h) * LANES, LANES)
        o_ref[:, pl.ds(qcol, LANES)] = (rot(qcol) * q_scale).astype(o_ref.dtype)
        k = rot(kcol)
        o_ref[:, pl.ds(kcol, LANES)] = k.astype(o_ref.dtype)
        kmean_ref[0, :, pl.ds(qcol, LANES)] = jnp.mean(k, axis=0, keepdims=True)
        return carry

    lax.fori_loop(0, n_heads, head, 0)


def _rotary(qkvg, seq, n_heads):
    t = qkvg.shape[0]
    width = n_heads * ATTN_HEAD_DIM
    nq = seq // MOBA_BLOCK
    half = ROT_DIM // 2
    inv_freq = ROPE_THETA ** (-jnp.arange(half, dtype=F32) * 2.0 / ROT_DIM)
    ang = jnp.arange(seq).astype(F32)[:, None] * inv_freq[None, :]
    cos, sin = jnp.cos(ang), jnp.sin(ang)
    zeros = lambda n: jnp.zeros((seq, n), F32)
    cos_t = jnp.concatenate([cos, cos, jnp.ones((seq, LANES - ROT_DIM), F32)], axis=1)
    sin_lo = jnp.concatenate([-sin, zeros(LANES - half)], axis=1)
    sin_hi = jnp.concatenate([zeros(half), sin, zeros(LANES - ROT_DIM)], axis=1)
    table_spec = pl.BlockSpec((MOBA_BLOCK, LANES), lambda r: (r % nq, 0))
    return pl.pallas_call(
        functools.partial(_rotary_kernel, n_heads=n_heads, q_scale=ATTN_HEAD_DIM ** -0.5),
        grid=(t // MOBA_BLOCK,),
        in_specs=[pl.BlockSpec((MOBA_BLOCK, 2 * width), lambda r: (r, 0)),
                  table_spec, table_spec, table_spec],
        out_specs=[pl.BlockSpec((MOBA_BLOCK, 2 * width), lambda r: (r, 0)),
                   pl.BlockSpec((1, 1, width), lambda r: (r, 0, 0))],
        out_shape=[jax.ShapeDtypeStruct((t, 2 * width), BF16),
                   jax.ShapeDtypeStruct((t // MOBA_BLOCK, 1, width), F32)],
        compiler_params=_params("parallel"),
        name="rotary_kmean",
    )(qkvg, cos_t, sin_lo, sin_hi)


def _attn_kernel(q_ref, k_ref, v_ref, g_ref, kmean_ref, o_ref, bias_scr, *, nq):
    i = pl.program_id(2)
    blk = MOBA_BLOCK
    q = q_ref[...]

    gate = lax.dot_general(kmean_ref[0], q.astype(F32), NT_DIMS, preferred_element_type=F32)
    n_idx = lax.broadcasted_iota(jnp.int32, (nq, blk), 0)
    beaten = jnp.zeros((nq, blk), F32)
    for m in range(nq):
        row = gate[m:m + 1, :]
        beats = (row > gate) | ((row == gate) & (m < n_idx))
        beaten = beaten + jnp.where(beats & (m < i), 1.0, 0.0)
    chosen = (n_idx < i) & (beaten < MOBA_TOPK)
    bias_scr[...] = jnp.where(chosen, 0.0, -jnp.inf)

    def scores(j):
        rows = pl.ds(pl.multiple_of(j * blk, blk), blk)
        s = lax.dot_general(k_ref[rows, :], q, NT_DIMS, preferred_element_type=F32)
        return s, v_ref[rows, :]

    s, v = scores(i)
    key_pos = lax.broadcasted_iota(jnp.int32, (blk, blk), 0)
    qry_pos = lax.broadcasted_iota(jnp.int32, (blk, blk), 1)
    s = jnp.where(key_pos <= qry_pos, s, -jnp.inf)
    m0 = jnp.max(s, axis=0, keepdims=True)
    p = jnp.exp(s - m0)
    l0 = jnp.sum(p, axis=0, keepdims=True)
    acc0 = lax.dot_general(v, p.astype(v.dtype), TN_DIMS, preferred_element_type=F32)

    def past_block(j, carry):
        m_run, l_run, acc = carry
        s, v = scores(j)
        s = s + bias_scr[pl.ds(j, 1), :]
        m_new = jnp.maximum(m_run, jnp.max(s, axis=0, keepdims=True))
        alpha = jnp.exp(m_run - m_new)
        p = jnp.exp(s - m_new)
        l_new = alpha * l_run + jnp.sum(p, axis=0, keepdims=True)
        acc = alpha * acc + lax.dot_general(v, p.astype(v.dtype), TN_DIMS, preferred_element_type=F32)
        return m_new, l_new, acc

    _, l_fin, acc = lax.fori_loop(0, i, past_block, (m0, l0, acc0))
    out = (acc / l_fin).T
    g = g_ref[...].astype(F32)
    o_ref[...] = (out * (g * jax.nn.sigmoid(g))).astype(o_ref.dtype)


def _attention(qk_rot, qkvg, kmean, bsz, seq, n_heads):
    t = qk_rot.shape[0]
    nq = seq // MOBA_BLOCK
    hd = ATTN_HEAD_DIM
    width = n_heads * hd
    kmean = kmean.reshape(bsz, nq, width)
    return pl.pallas_call(
        functools.partial(_attn_kernel, nq=nq),
        grid=(bsz, n_heads, nq),
        in_specs=[pl.BlockSpec((MOBA_BLOCK, hd), lambda b, h, i: (b * nq + i, h)),
                  pl.BlockSpec((seq, hd), lambda b, h, i: (b, n_heads + h)),
                  pl.BlockSpec((seq, hd), lambda b, h, i: (b, 2 * n_heads + h)),
                  pl.BlockSpec((MOBA_BLOCK, hd), lambda b, h, i: (b * nq + i, 3 * n_heads + h)),
                  pl.BlockSpec((1, nq, hd), lambda b, h, i: (b, 0, h))],
        out_specs=pl.BlockSpec((MOBA_BLOCK, hd), lambda b, h, i: (b * nq + i, h)),
        out_shape=jax.ShapeDtypeStruct((t, width), BF16),
        scratch_shapes=[pltpu.VMEM((nq, MOBA_BLOCK), F32)],
        compiler_params=_params("parallel", "parallel", "arbitrary"),
        name="moba_attention",
    )(qk_rot, qk_rot, qkvg, qkvg, kmean)


def _split3_bf16(a):
    hi = a.astype(BF16)
    r = a - hi.astype(F32)
    mid = r.astype(BF16)
    lo = (r - mid.astype(F32)).astype(BF16)
    return hi, mid, lo


def _ssd_kernel(z_ref, xs_ref, bc_ref, dt_ref, convw_ref, convb_ref, dtb_ref, alog_ref, dskip_ref, normw_ref,
                o_ref, carry_scr, ext_scr, xc_scr, bcc_scr, y_scr, state_scr, *, n_groups, heads_per_group):
    c = pl.program_id(1)
    L = SSD_CHUNK
    d_inner = xs_ref.shape[1]
    gw = heads_per_group * SSD_HEAD_DIM
    strip = 512

    @pl.when(c == 0)
    def _():
        carry_scr[...] = jnp.zeros_like(carry_scr)
        state_scr[...] = jnp.zeros_like(state_scr)

    def conv_strip(src_ref, src_col, conv_col, dst_ref, dst_col, width):
        cur = src_ref[:, pl.ds(src_col, width)].astype(F32)
        ext_scr[0:CONV_CARRY_ROWS, 0:width] = carry_scr[:, pl.ds(conv_col, width)]
        ext_scr[CONV_CARRY_ROWS:CONV_CARRY_ROWS + L, 0:width] = cur
        carry_scr[:, pl.ds(conv_col, width)] = cur[L - CONV_CARRY_ROWS:, :]
        acc = jnp.broadcast_to(convb_ref[:, pl.ds(conv_col, width)], (L, width))
        for tap in range(SSD_CONV):
            off = CONV_CARRY_ROWS - (SSD_CONV - 1) + tap
            acc = acc + ext_scr[off:off + L, 0:width] * convw_ref[tap:tap + 1, pl.ds(conv_col, width)]
        dst_ref[:, pl.ds(dst_col, width)] = (acc * jax.nn.sigmoid(acc)).astype(dst_ref.dtype)

    for s in range(d_inner // strip):
        conv_strip(xs_ref, s * strip, s * strip, xc_scr, s * strip, strip)
    for s in range(bc_ref.shape[1] // strip):
        conv_strip(bc_ref, s * strip, d_inner + s * strip, bcc_scr, s * strip, strip)

    xdt = dt_ref[...] + dtb_ref[...]
    dt = jnp.maximum(xdt, 0.0) + jnp.log1p(jnp.exp(-jnp.abs(xdt)))
    a = dt * (-jnp.exp(alog_ref[...]))
    t_row = lax.broadcasted_iota(jnp.int32, (L, L), 0)
    t_col = lax.broadcasted_iota(jnp.int32, (L, L), 1)
    causal = t_col <= t_row
    tril = jnp.where(causal, 1.0, 0.0).astype(BF16)
    acum = sum(jnp.dot(tril, part, preferred_element_type=F32) for part in _split3_bf16(a))
    a_tot = acum[L - 1:L, :]
    log_dt = jnp.log(dt)
    src_t = (acum - log_dt).T
    carry_exp = a_tot - acum + log_dt
    state_decay = jnp.exp(a_tot)

    lane = lax.broadcasted_iota(jnp.int32, (L, LANES), 1)
    low_half = lane < SSD_HEAD_DIM
    n_bc = n_groups * SSD_STATE

    for g in range(n_groups):
        b_g = bcc_scr[:, g * SSD_STATE:(g + 1) * SSD_STATE]
        c_g = bcc_scr[:, n_bc + g * SSD_STATE:n_bc + (g + 1) * SSD_STATE]
        cb = lax.dot_general(c_g, b_g, NT_DIMS, preferred_element_type=F32)
        state_g = state_scr[g]
        y_off = jnp.dot(c_g, state_g.astype(BF16), preferred_element_type=F32)
        for pair in range(heads_per_group // 2):
            col0 = g * gw + pair * LANES
            x_pair = xc_scr[:, col0:col0 + LANES]
            x_lo = jnp.where(low_half, x_pair, 0.0).astype(BF16)
            x_hi = jnp.where(low_half, 0.0, x_pair).astype(BF16)
            h0 = g * heads_per_group + 2 * pair
            y_diag = jnp.zeros((L, LANES), F32)
            for h, x_half in ((h0, x_lo), (h0 + 1, x_hi)):
                seg = acum[:, h:h + 1] - src_t[h:h + 1, :]
                w = (cb * jnp.exp(jnp.where(causal, seg, -jnp.inf))).astype(BF16)
                y_diag = y_diag + jnp.dot(w, x_half, preferred_element_type=F32)
            in_decay = jnp.where(low_half, jnp.exp(acum[:, h0:h0 + 1]), jnp.exp(acum[:, h0 + 1:h0 + 2]))
            y = (y_diag + in_decay * y_off[:, pair * LANES:(pair + 1) * LANES]
                 + dskip_ref[:, col0:col0 + LANES] * x_pair)
            zg = z_ref[:, col0:col0 + LANES].astype(F32)
            y_scr[:, col0:col0 + LANES] = y * (zg * jax.nn.sigmoid(zg))
            to_end = jnp.where(low_half, jnp.exp(carry_exp[:, h0:h0 + 1]), jnp.exp(carry_exp[:, h0 + 1:h0 + 2]))
            xc_scr[:, col0:col0 + LANES] = x_pair * to_end
            sd = jnp.where(low_half[0:1, :], state_decay[:, h0:h0 + 1], state_decay[:, h0 + 1:h0 + 2])
            state_scr[g, :, pair * LANES:(pair + 1) * LANES] = state_g[:, pair * LANES:(pair + 1) * LANES] * sd
        x_end = xc_scr[:, g * gw:(g + 1) * gw].astype(BF16)
        state_scr[g] = state_scr[g] + lax.dot_general(b_g, x_end, TN_DIMS, preferred_element_type=F32)
        yg = y_scr[:, g * gw:(g + 1) * gw]
        ms = jnp.mean(yg * yg, axis=-1, keepdims=True)
        o_ref[:, g * gw:(g + 1) * gw] = (yg * lax.rsqrt(ms + NORM_EPS)
                                         * normw_ref[:, g * gw:(g + 1) * gw]).astype(o_ref.dtype)


def _ssd(zx, dt_raw, conv_w, conv_b, dt_bias, a_log, d_skip, norm_w, bsz, seq, d_inner, n_heads, n_groups):
    t = zx.shape[0]
    nc = seq // SSD_CHUNK
    bc_w = 2 * n_groups * SSD_STATE
    conv_dim = d_inner + bc_w
    heads_per_group = n_heads // n_groups
    gw = heads_per_group * SSD_HEAD_DIM
    assert heads_per_group % 2 == 0 and d_inner % 512 == 0 and bc_w % 512 == 0 and d_inner % bc_w == 0
    assert n_heads <= LANES
    pad_heads = lambda v: jnp.pad(v.astype(F32), (0, LANES - n_heads)).reshape(1, LANES)
    row = lambda b, c: (b * nc + c, 0)
    const = lambda b, c: (0, 0)
    return pl.pallas_call(
        functools.partial(_ssd_kernel, n_groups=n_groups, heads_per_group=heads_per_group),
        grid=(bsz, nc),
        in_specs=[pl.BlockSpec((SSD_CHUNK, d_inner), row),
                  pl.BlockSpec((SSD_CHUNK, d_inner), lambda b, c: (b * nc + c, 1)),
                  pl.BlockSpec((SSD_CHUNK, bc_w), lambda b, c: (b * nc + c, 2 * d_inner // bc_w)),
                  pl.BlockSpec((SSD_CHUNK, LANES), row),
                  pl.BlockSpec((SSD_CONV, conv_dim), const),
                  pl.BlockSpec((1, conv_dim), const),
                  pl.BlockSpec((1, LANES), const),
                  pl.BlockSpec((1, LANES), const),
                  pl.BlockSpec((1, d_inner), const),
                  pl.BlockSpec((1, d_inner), const)],
        out_specs=pl.BlockSpec((SSD_CHUNK, d_inner), row),
        out_shape=jax.ShapeDtypeStruct((t, d_inner), BF16),
        scratch_shapes=[pltpu.VMEM((CONV_CARRY_ROWS, conv_dim), F32),
                        pltpu.VMEM((CONV_CARRY_ROWS + SSD_CHUNK, 512), F32),
                        pltpu.VMEM((SSD_CHUNK, d_inner), F32),
                        pltpu.VMEM((SSD_CHUNK, bc_w), BF16),
                        pltpu.VMEM((SSD_CHUNK, d_inner), F32),
                        pltpu.VMEM((n_groups, SSD_STATE, gw), F32)],
        compiler_params=_params("parallel", "arbitrary"),
        name="ssd_scan",
    )(zx, zx, zx, dt_raw, conv_w, conv_b.reshape(1, conv_dim), pad_heads(dt_bias), pad_heads(a_log),
      jnp.repeat(d_skip.astype(F32), SSD_HEAD_DIM).reshape(1, d_inner), norm_w.reshape(1, d_inner))


def _merge_kernel(oa_ref, wa_ref, ys_ref, ws_ref, ga_ref, gs_ref, ba_ref, bs_ref, o_ref):
    y_attn = jnp.dot(oa_ref[...], wa_ref[...], preferred_element_type=F32)
    y_ssd = jnp.dot(ys_ref[...], ws_ref[...], preferred_element_type=F32)
    gate_a = jax.nn.sigmoid(ga_ref[...].astype(F32) + ba_ref[...])
    gate_s = jax.nn.sigmoid(gs_ref[...].astype(F32) + bs_ref[...])
    o_ref[...] = (gate_a * y_attn + gate_s * y_ssd).astype(o_ref.dtype)


def _merge(o_attn, w_attn_out, y_ssd, w_ssd_out, g_merge, gate_bias):
    t, wa = o_attn.shape
    ws = y_ssd.shape[1]
    d = w_attn_out.shape[1]
    tm, tn = _tile(t, 1024), _tile(d, 512)
    nj = d // tn
    bias = gate_bias.astype(F32).reshape(1, 2 * d)
    return pl.pallas_call(
        _merge_kernel,
        grid=(t // tm, nj),
        in_specs=[pl.BlockSpec((tm, wa), lambda i, j: (i, 0)),
                  pl.BlockSpec((wa, tn), lambda i, j: (0, j)),
                  pl.BlockSpec((tm, ws), lambda i, j: (i, 0)),
                  pl.BlockSpec((ws, tn), lambda i, j: (0, j)),
                  pl.BlockSpec((tm, tn), lambda i, j: (i, j)),
                  pl.BlockSpec((tm, tn), lambda i, j: (i, nj + j)),
                  pl.BlockSpec((1, tn), lambda i, j: (0, j)),
                  pl.BlockSpec((1, tn), lambda i, j: (0, nj + j))],
        out_specs=pl.BlockSpec((tm, tn), lambda i, j: (i, j)),
        out_shape=jax.ShapeDtypeStruct((t, d), BF16),
        compiler_params=_params("parallel", "arbitrary"),
        name="branch_merge",
    )(o_attn, w_attn_out, y_ssd, w_ssd_out, g_merge, g_merge, bias, bias)


def _out_kernel(m_ref, w_ref, x_ref, fw_ref, o_ref, ssq_scr, *, tn, d_model, final_norm):
    j = pl.program_id(1)

    @pl.when(j == 0)
    def _():
        ssq_scr[...] = jnp.zeros_like(ssq_scr)

    h = x_ref[...] + jnp.dot(m_ref[...], w_ref[...], preferred_element_type=F32)
    o_ref[:, pl.ds(pl.multiple_of(j * tn, tn), tn)] = h
    ssq_scr[...] += jnp.sum(h * h, axis=-1, keepdims=True)

    if final_norm:
        @pl.when(j == pl.num_programs(1) - 1)
        def _():
            inv = lax.rsqrt(ssq_scr[...] * (1.0 / d_model) + NORM_EPS)
            o_ref[...] = o_ref[...] * inv * fw_ref[...]


def _out_proj(merged, w_out, x, final_norm_w, final_norm):
    t, d = x.shape
    tm, tn = _tile(t, 512), _tile(d, 512)
    return pl.pallas_call(
        functools.partial(_out_kernel, tn=tn, d_model=d, final_norm=final_norm),
        grid=(t // tm, d // tn),
        in_specs=[pl.BlockSpec((tm, d), lambda i, j: (i, 0)),
                  pl.BlockSpec((d, tn), lambda i, j: (0, j)),
                  pl.BlockSpec((tm, tn), lambda i, j: (i, j)),
                  pl.BlockSpec((1, d), lambda i, j: (0, 0))],
        out_specs=pl.BlockSpec((tm, d), lambda i, j: (i, 0)),
        out_shape=jax.ShapeDtypeStruct((t, d), F32),
        scratch_shapes=[pltpu.VMEM((tm, 1), F32)],
        compiler_params=_params("parallel", "arbitrary"),
        name="out_proj_norm",
    )(merged, w_out, x, final_norm_w.astype(F32).reshape(1, d))


def kernel(x, norm_w, w_in, conv_w, conv_b, dt_bias, a_log, d_skip, ssd_norm_w, w_attn_out, w_ssd_out,
           gate_bias, w_out, final_norm_w):
    bsz, seq, d_model = x.shape
    depth = norm_w.shape[0]
    attn_w = w_attn_out.shape[1]
    d_inner = w_ssd_out.shape[1]
    n_ssd_heads = a_log.shape[1]
    conv_dim = conv_w.shape[2]
    n_groups = (conv_dim - d_inner) // (2 * SSD_STATE)
    n_attn_heads = attn_w // ATTN_HEAD_DIM
    assert seq % MOBA_BLOCK == 0 and seq % SSD_CHUNK == 0
    c_zx, c_dt = 4 * attn_w, 4 * attn_w + d_inner + conv_dim
    c_gm = c_dt + n_ssd_heads

    h = x.reshape(bsz * seq, d_model)
    for l in range(depth):
        w = w_in[l]
        u = _rmsnorm(h, norm_w[l], BF16)
        qkvg = _matmul(u, w[:, :c_zx].astype(BF16), BF16, "in_proj_attn")
        zx = _matmul(u, w[:, c_zx:c_dt].astype(BF16), BF16, "in_proj_ssd")
        w_dt = jnp.pad(w[:, c_dt:c_gm], ((0, 0), (0, LANES - n_ssd_heads))).astype(BF16)
        dt_raw = _matmul(u, w_dt, F32, "in_proj_dt")
        g_merge = _matmul(u, w[:, c_gm:].astype(BF16), BF16, "in_proj_gate")

        qk_rot, kmean = _rotary(qkvg, seq, n_attn_heads)
        o_attn = _attention(qk_rot, qkvg, kmean, bsz, seq, n_attn_heads)
        y_ssd = _ssd(zx, dt_raw, conv_w[l], conv_b[l], dt_bias[l], a_log[l], d_skip[l], ssd_norm_w[l],
                     bsz, seq, d_inner, n_ssd_heads, n_groups)
        merged = _merge(o_attn, w_attn_out[l].astype(BF16), y_ssd, w_ssd_out[l].astype(BF16), g_merge,
                        gate_bias[l])
        h = _out_proj(merged, w_out[l].astype(BF16), h, final_norm_w, final_norm=(l == depth - 1))
    return h.reshape(bsz, seq, d_model)
```

```python
import functools
import math

import jax
import jax.numpy as jnp
from jax import lax
from jax.experimental import pallas as pl
from jax.experimental.pallas import tpu as pltpu

F32 = jnp.float32
BF16 = jnp.bfloat16

NORM_EPS = 1e-6
ATTN_HEAD_DIM = 128
ROT_DIM = ATTN_HEAD_DIM // 4
ROPE_THETA = 500000.0
MOBA_BLOCK = 256
MOBA_TOPK = 3
SSD_HEAD_DIM = 64
SSD_STATE = 128
SSD_CONV = 4
SSD_CHUNK = 256
CONV_CARRY_ROWS = 8
LOG2_E = 1.4426950408889634

V7X_VMEM_LIMIT_BYTES = 56 * 1024 * 1024
LANES = 128

NT_DIMS = (((1,), (1,)), ((), ()))
TN_DIMS = (((0,), (0,)), ((), ()))


def _params(*semantics):
    return pltpu.CompilerParams(dimension_semantics=semantics,
                                vmem_limit_bytes=V7X_VMEM_LIMIT_BYTES)


def _tile(n, pref):
    for t in range(min(n, pref), 0, -LANES):
        if n % t == 0:
            return t
    raise ValueError(f"no lane-aligned tile for {n}")


def _rmsnorm_kernel(x_ref, w_ref, o_ref):
    x = x_ref[...]
    ms = jnp.mean(x * x, axis=-1, keepdims=True)
    o_ref[...] = (x * lax.rsqrt(ms + NORM_EPS) * w_ref[...]).astype(o_ref.dtype)


def _rmsnorm(x, w, out_dtype):
    t, d = x.shape
    tm = _tile(t, 256)
    return pl.pallas_call(
        _rmsnorm_kernel,
        grid=(t // tm,),
        in_specs=[pl.BlockSpec((tm, d), lambda i: (i, 0)),
                  pl.BlockSpec((1, d), lambda i: (0, 0))],
        out_specs=pl.BlockSpec((tm, d), lambda i: (i, 0)),
        out_shape=jax.ShapeDtypeStruct((t, d), out_dtype),
        compiler_params=_params("parallel"),
        name="rmsnorm",
    )(x, w.reshape(1, d))


def _matmul_kernel(x_ref, w_ref, o_ref):
    o_ref[...] = jnp.dot(x_ref[...], w_ref[...], preferred_element_type=F32).astype(o_ref.dtype)


def _matmul(x, w, out_dtype, name, col0=0, n=None):
    m, k = x.shape
    n = w.shape[1] - col0 if n is None else n
    tm, tn = _tile(m, 1024), _tile(math.gcd(n, col0), 1024)
    j0 = col0 // tn
    return pl.pallas_call(
        _matmul_kernel,
        grid=(m // tm, n // tn),
        in_specs=[pl.BlockSpec((tm, k), lambda i, j: (i, 0)),
                  pl.BlockSpec((k, tn), lambda i, j: (0, j0 + j))],
        out_specs=pl.BlockSpec((tm, tn), lambda i, j: (i, j)),
        out_shape=jax.ShapeDtypeStruct((m, n), out_dtype),
        compiler_params=_params("parallel", "arbitrary"),
        name=name,
    )(x, w)


def _rotary_kernel(qk_ref, cos_ref, sin_lo_ref, sin_hi_ref, o_ref, kmean_ref, *, n_heads, q_scale):
    cos, sin_lo, sin_hi = cos_ref[...], sin_lo_ref[...], sin_hi_ref[...]
    half = ROT_DIM // 2

    def rot(col):
        t = qk_ref[:, pl.ds(col, LANES)].astype(F32)
        return (t * cos + pltpu.roll(t, LANES - half, 1) * sin_lo + pltpu.roll(t, half, 1) * sin_hi)

    def head(h, carry):
        qcol = pl.multiple_of(h * LANES, LANES)
        kcol = pl.multiple_of((n_heads + h) * LANES, LANES)
        o_ref[:, pl.ds(qcol, LANES)] = (rot(qcol) * q_scale).astype(o_ref.dtype)
        k = rot(kcol)
        o_ref[:, pl.ds(kcol, LANES)] = k.astype(o_ref.dtype)
        kmean_ref[0, :, pl.ds(qcol, LANES)] = jnp.mean(k, axis=0, keepdims=True)
        return carry

    lax.fori_loop(0, n_heads, head, 0)


def _rotary(qkvg, seq, n_heads):
    t = qkvg.shape[0]
    width = n_heads * ATTN_HEAD_DIM
    nq = seq // MOBA_BLOCK
    half = ROT_DIM // 2
    inv_freq = ROPE_THETA ** (-jnp.arange(half, dtype=F32) * 2.0 / ROT_DIM)
    ang = jnp.arange(seq).astype(F32)[:, None] * inv_freq[None, :]
    cos, sin = jnp.cos(ang), jnp.sin(ang)
    zeros = lambda n: jnp.zeros((seq, n), F32)
    cos_t = jnp.concatenate([cos, cos, jnp.ones((seq, LANES - ROT_DIM), F32)], axis=1)
    sin_lo = jnp.concatenate([-sin, zeros(LANES - half)], axis=1)
    sin_hi = jnp.concatenate([zeros(half), sin, zeros(LANES - ROT_DIM)], axis=1)
    table_spec = pl.BlockSpec((MOBA_BLOCK, LANES), lambda r: (r % nq, 0))
    return pl.pallas_call(
        functools.partial(_rotary_kernel, n_heads=n_heads, q_scale=ATTN_HEAD_DIM ** -0.5 * LOG2_E),
        grid=(t // MOBA_BLOCK,),
        in_specs=[pl.BlockSpec((MOBA_BLOCK, 2 * width), lambda r: (r, 0)),
                  table_spec, table_spec, table_spec],
        out_specs=[pl.BlockSpec((MOBA_BLOCK, 2 * width), lambda r: (r, 0)),
                   pl.BlockSpec((1, 1, width), lambda r: (r, 0, 0))],
        out_shape=[jax.ShapeDtypeStruct((t, 2 * width), BF16),
                   jax.ShapeDtypeStruct((t // MOBA_BLOCK, 1, width), F32)],
        compiler_params=_params("parallel"),
        name="rotary_kmean",
    )(qkvg, cos_t, sin_lo, sin_hi)


def _attn_kernel(q_ref, k_ref, v_ref, g_ref, kmean_ref, o_ref, bias_scr, acc_scr, *, nq, n_chains):
    i = pl.program_id(2)
    blk = MOBA_BLOCK
    hd = ATTN_HEAD_DIM
    n_idx = lax.broadcasted_iota(jnp.int32, (nq, blk), 0)
    key_pos = lax.broadcasted_iota(jnp.int32, (blk, blk), 0)
    qry_pos = lax.broadcasted_iota(jnp.int32, (blk, blk), 1)
    heads = [slice(c * hd, (c + 1) * hd) for c in range(n_chains)]
    qs = [q_ref[:, hs] for hs in heads]

    def scores(c, j):
        rows = pl.ds(pl.multiple_of(j * blk, blk), blk)
        s = lax.dot_general(k_ref[rows, heads[c]], qs[c], NT_DIMS, preferred_element_type=F32)
        return s, v_ref[rows, heads[c]]

    gates = [lax.dot_general(kmean_ref[0, :, heads[c]], qs[c].astype(F32), NT_DIMS, preferred_element_type=F32)
             for c in range(n_chains)]
    sv = [scores(c, i) for c in range(n_chains)]
    stats, pv_in = [], []
    for c in range(n_chains):
        gate = gates[c]
        beaten = jnp.zeros((nq, blk), F32)
        for m in range(nq):
            row = gate[m:m + 1, :]
            beats = (row > gate) | ((row == gate) & (m < n_idx))
            beaten = beaten + jnp.where(beats & (m < i), 1.0, 0.0)
        chosen = (n_idx < i) & (beaten < MOBA_TOPK)
        bias_scr[c] = jnp.where(chosen, 0.0, -jnp.inf)

        s = jnp.where(key_pos <= qry_pos, sv[c][0], -jnp.inf)
        m0 = jnp.max(s, axis=0, keepdims=True)
        p = jnp.exp2(s - m0)
        stats.append((m0, jnp.sum(p, axis=0, keepdims=True)))
        pv_in.append(p.astype(BF16))
    for c in range(n_chains):
        acc_scr[c] = lax.dot_general(sv[c][1], pv_in[c], TN_DIMS, preferred_element_type=F32)

    def past_block(j, carry):
        sv = [scores(c, j) for c in range(n_chains)]
        new, pv_in = [], []
        for c in range(n_chains):
            m_run, l_run = carry[c]
            s = sv[c][0] + bias_scr[c, pl.ds(j, 1), :]
            m_new = jnp.maximum(m_run, jnp.max(s, axis=0, keepdims=True))
            alpha = jnp.exp2(m_run - m_new)
            p = jnp.exp2(s - m_new)
            new.append((m_new, alpha * l_run + jnp.sum(p, axis=0, keepdims=True)))
            pv_in.append((alpha, p.astype(BF16)))
        for c in range(n_chains):
            alpha, p = pv_in[c]
            acc_scr[c] = alpha * acc_scr[c] + lax.dot_general(sv[c][1], p, TN_DIMS, preferred_element_type=F32)
        return tuple(new)

    stats = lax.fori_loop(0, i, past_block, tuple(stats))
    for c in range(n_chains):
        out = (acc_scr[c] / stats[c][1]).T
        g = g_ref[:, heads[c]].astype(F32)
        o_ref[:, heads[c]] = (out * (g * jax.nn.sigmoid(g))).astype(o_ref.dtype)


ATTN_HEADS_PER_STEP = 4


def _attention(qk_rot, qkvg, kmean, bsz, seq, n_heads):
    t = qk_rot.shape[0]
    nq = seq // MOBA_BLOCK
    width = n_heads * ATTN_HEAD_DIM
    hp = ATTN_HEADS_PER_STEP if n_heads % ATTN_HEADS_PER_STEP == 0 else 1
    n_hg, cw = n_heads // hp, hp * ATTN_HEAD_DIM
    kmean = kmean.reshape(bsz, nq, width)
    return pl.pallas_call(
        functools.partial(_attn_kernel, nq=nq, n_chains=hp),
        grid=(bsz, n_hg, nq),
        in_specs=[pl.BlockSpec((MOBA_BLOCK, cw), lambda b, h, i: (b * nq + i, h)),
                  pl.BlockSpec((seq, cw), lambda b, h, i: (b, n_hg + h)),
                  pl.BlockSpec((seq, cw), lambda b, h, i: (b, 2 * n_hg + h)),
                  pl.BlockSpec((MOBA_BLOCK, cw), lambda b, h, i: (b * nq + i, 3 * n_hg + h)),
                  pl.BlockSpec((1, nq, cw), lambda b, h, i: (b, 0, h))],
        out_specs=pl.BlockSpec((MOBA_BLOCK, cw), lambda b, h, i: (b * nq + i, h)),
        out_shape=jax.ShapeDtypeStruct((t, width), BF16),
        scratch_shapes=[pltpu.VMEM((hp, nq, MOBA_BLOCK), F32),
                        pltpu.VMEM((hp, ATTN_HEAD_DIM, MOBA_BLOCK), F32)],
        compiler_params=_params("parallel", "parallel", "arbitrary"),
        name="moba_attention",
    )(qk_rot, qk_rot, qkvg, qkvg, kmean)


def _split3_bf16(a):
    hi = a.astype(BF16)
    r = a - hi.astype(F32)
    mid = r.astype(BF16)
    lo = (r - mid.astype(F32)).astype(BF16)
    return hi, mid, lo


def _ssd_kernel(z_ref, xs_ref, bc_ref, dt_ref, convw_ref, convb_ref, dtb_ref, alog_ref, dskip_ref, normw_ref,
                o_ref, carry_scr, ext_scr, xc_scr, bcc_scr, y_scr, state_scr, *, n_groups, heads_per_group):
    c = pl.program_id(1)
    L = SSD_CHUNK
    d_inner = xs_ref.shape[1]
    gw = heads_per_group * SSD_HEAD_DIM
    strip = 512

    @pl.when(c == 0)
    def _():
        carry_scr[...] = jnp.zeros_like(carry_scr)
        state_scr[...] = jnp.zeros_like(state_scr)

    def conv_strip(src_ref, src_col, conv_col, dst_ref, dst_col, width):
        cur = src_ref[:, pl.ds(src_col, width)].astype(F32)
        ext_scr[0:CONV_CARRY_ROWS, 0:width] = carry_scr[:, pl.ds(conv_col, width)]
        ext_scr[CONV_CARRY_ROWS:CONV_CARRY_ROWS + L, 0:width] = cur
        carry_scr[:, pl.ds(conv_col, width)] = cur[L - CONV_CARRY_ROWS:, :]
        acc = jnp.broadcast_to(convb_ref[:, pl.ds(conv_col, width)], (L, width))
        for tap in range(SSD_CONV):
            off = CONV_CARRY_ROWS - (SSD_CONV - 1) + tap
            acc = acc + ext_scr[off:off + L, 0:width] * convw_ref[tap:tap + 1, pl.ds(conv_col, width)]
        dst_ref[:, pl.ds(dst_col, width)] = (acc * jax.nn.sigmoid(acc)).astype(dst_ref.dtype)

    for s in range(d_inner // strip):
        conv_strip(xs_ref, s * strip, s * strip, xc_scr, s * strip, strip)
    for s in range(bc_ref.shape[1] // strip):
        conv_strip(bc_ref, s * strip, d_inner + s * strip, bcc_scr, s * strip, strip)

    xdt = dt_ref[...] + dtb_ref[...]
    dt = jnp.maximum(xdt, 0.0) + jnp.log1p(jnp.exp(-jnp.abs(xdt)))
    a = dt * (-jnp.exp(alog_ref[...]))
    t_row = lax.broadcasted_iota(jnp.int32, (L, L), 0)
    t_col = lax.broadcasted_iota(jnp.int32, (L, L), 1)
    causal = t_col <= t_row
    tril = jnp.where(causal, 1.0, 0.0).astype(BF16)
    acum = sum(jnp.dot(tril, part, preferred_element_type=F32) for part in _split3_bf16(a))
    a_tot = acum[L - 1:L, :]
    log_dt = jnp.log(dt)
    src_t = (acum - log_dt).T
    carry_exp = a_tot - acum + log_dt
    state_decay = jnp.exp(a_tot)

    lane = lax.broadcasted_iota(jnp.int32, (L, LANES), 1)
    low_half = lane < SSD_HEAD_DIM
    n_bc = n_groups * SSD_STATE

    for g in range(n_groups):
        b_g = bcc_scr[:, g * SSD_STATE:(g + 1) * SSD_STATE]
        c_g = bcc_scr[:, n_bc + g * SSD_STATE:n_bc + (g + 1) * SSD_STATE]
        cb = lax.dot_general(c_g, b_g, NT_DIMS, preferred_element_type=F32)
        state_g = state_scr[g]
        y_off = jnp.dot(c_g, state_g.astype(BF16), preferred_element_type=F32)
        for pair in range(heads_per_group // 2):
            col0 = g * gw + pair * LANES
            x_pair = xc_scr[:, col0:col0 + LANES]
            x_lo = jnp.where(low_half, x_pair, 0.0).astype(BF16)
            x_hi = jnp.where(low_half, 0.0, x_pair).astype(BF16)
            h0 = g * heads_per_group + 2 * pair
            y_diag = jnp.zeros((L, LANES), F32)
            for h, x_half in ((h0, x_lo), (h0 + 1, x_hi)):
                seg = acum[:, h:h + 1] - src_t[h:h + 1, :]
                w = (cb * jnp.exp(jnp.where(causal, seg, -jnp.inf))).astype(BF16)
                y_diag = y_diag + jnp.dot(w, x_half, preferred_element_type=F32)
            in_decay = jnp.where(low_half, jnp.exp(acum[:, h0:h0 + 1]), jnp.exp(acum[:, h0 + 1:h0 + 2]))
            y = (y_diag + in_decay * y_off[:, pair * LANES:(pair + 1) * LANES]
                 + dskip_ref[:, col0:col0 + LANES] * x_pair)
            zg = z_ref[:, col0:col0 + LANES].astype(F32)
            y_scr[:, col0:col0 + LANES] = y * (zg * jax.nn.sigmoid(zg))
            to_end = jnp.where(low_half, jnp.exp(carry_exp[:, h0:h0 + 1]), jnp.exp(carry_exp[:, h0 + 1:h0 + 2]))
            xc_scr[:, col0:col0 + LANES] = x_pair * to_end
            sd = jnp.where(low_half[0:1, :], state_decay[:, h0:h0 + 1], state_decay[:, h0 + 1:h0 + 2])
            state_scr[g, :, pair * LANES:(pair + 1) * LANES] = state_g[:, pair * LANES:(pair + 1) * LANES] * sd
        x_end = xc_scr[:, g * gw:(g + 1) * gw].astype(BF16)
        state_scr[g] = state_scr[g] + lax.dot_general(b_g, x_end, TN_DIMS, preferred_element_type=F32)
        yg = y_scr[:, g * gw:(g + 1) * gw]
        ms = jnp.mean(yg * yg, axis=-1, keepdims=True)
        o_ref[:, g * gw:(g + 1) * gw] = (yg * lax.rsqrt(ms + NORM_EPS)
                                         * normw_ref[:, g * gw:(g + 1) * gw]).astype(o_ref.dtype)


def _ssd(zx, dt_raw, conv_w, conv_b, dt_bias, a_log, d_skip, norm_w, bsz, seq, d_inner, n_heads, n_groups):
    t = zx.shape[0]
    nc = seq // SSD_CHUNK
    bc_w = 2 * n_groups * SSD_STATE
    conv_dim = d_inner + bc_w
    heads_per_group = n_heads // n_groups
    gw = heads_per_group * SSD_HEAD_DIM
    assert heads_per_group % 2 == 0 and d_inner % 512 == 0 and bc_w % 512 == 0 and d_inner % bc_w == 0
    assert n_heads <= LANES
    pad_heads = lambda v: jnp.pad(v.astype(F32), (0, LANES - n_heads)).reshape(1, LANES)
    row = lambda b, c: (b * nc + c, 0)
    const = lambda b, c: (0, 0)
    return pl.pallas_call(
        functools.partial(_ssd_kernel, n_groups=n_groups, heads_per_group=heads_per_group),
        grid=(bsz, nc),
        in_specs=[pl.BlockSpec((SSD_CHUNK, d_inner), row),
                  pl.BlockSpec((SSD_CHUNK, d_inner), lambda b, c: (b * nc + c, 1)),
                  pl.BlockSpec((SSD_CHUNK, bc_w), lambda b, c: (b * nc + c, 2 * d_inner // bc_w)),
                  pl.BlockSpec((SSD_CHUNK, LANES), row),
                  pl.BlockSpec((SSD_CONV, conv_dim), const),
                  pl.BlockSpec((1, conv_dim), const),
                  pl.BlockSpec((1, LANES), const),
                  pl.BlockSpec((1, LANES), const),
                  pl.BlockSpec((1, d_inner), const),
                  pl.BlockSpec((1, d_inner), const)],
        out_specs=pl.BlockSpec((SSD_CHUNK, d_inner), row),
        out_shape=jax.ShapeDtypeStruct((t, d_inner), BF16),
        scratch_shapes=[pltpu.VMEM((CONV_CARRY_ROWS, conv_dim), F32),
                        pltpu.VMEM((CONV_CARRY_ROWS + SSD_CHUNK, 512), F32),
                        pltpu.VMEM((SSD_CHUNK, d_inner), F32),
                        pltpu.VMEM((SSD_CHUNK, bc_w), BF16),
                        pltpu.VMEM((SSD_CHUNK, d_inner), F32),
                        pltpu.VMEM((n_groups, SSD_STATE, gw), F32)],
        compiler_params=_params("parallel", "arbitrary"),
        name="ssd_scan",
    )(zx, zx, zx, dt_raw, conv_w, conv_b.reshape(1, conv_dim), pad_heads(dt_bias), pad_heads(a_log),
      jnp.repeat(d_skip.astype(F32), SSD_HEAD_DIM).reshape(1, d_inner), norm_w.reshape(1, d_inner))


def _merge_kernel(oa_ref, wa_ref, ys_ref, ws_ref, ga_ref, gs_ref, ba_ref, bs_ref, o_ref):
    y_attn = jnp.dot(oa_ref[...], wa_ref[...], preferred_element_type=F32)
    y_ssd = jnp.dot(ys_ref[...], ws_ref[...], preferred_element_type=F32)
    gate_a = jax.nn.sigmoid(ga_ref[...].astype(F32) + ba_ref[...])
    gate_s = jax.nn.sigmoid(gs_ref[...].astype(F32) + bs_ref[...])
    o_ref[...] = (gate_a * y_attn + gate_s * y_ssd).astype(o_ref.dtype)


def _merge(o_attn, w_attn_out, y_ssd, w_ssd_out, g_merge, gate_bias):
    t, wa = o_attn.shape
    ws = y_ssd.shape[1]
    d = w_attn_out.shape[1]
    tm, tn = _tile(t, 1024), _tile(d, 512)
    nj = d // tn
    bias = gate_bias.astype(F32).reshape(1, 2 * d)
    return pl.pallas_call(
        _merge_kernel,
        grid=(t // tm, nj),
        in_specs=[pl.BlockSpec((tm, wa), lambda i, j: (i, 0)),
                  pl.BlockSpec((wa, tn), lambda i, j: (0, j)),
                  pl.BlockSpec((tm, ws), lambda i, j: (i, 0)),
                  pl.BlockSpec((ws, tn), lambda i, j: (0, j)),
                  pl.BlockSpec((tm, tn), lambda i, j: (i, j)),
                  pl.BlockSpec((tm, tn), lambda i, j: (i, nj + j)),
                  pl.BlockSpec((1, tn), lambda i, j: (0, j)),
                  pl.BlockSpec((1, tn), lambda i, j: (0, nj + j))],
        out_specs=pl.BlockSpec((tm, tn), lambda i, j: (i, j)),
        out_shape=jax.ShapeDtypeStruct((t, d), BF16),
        compiler_params=_params("parallel", "arbitrary"),
        name="branch_merge",
    )(o_attn, w_attn_out, y_ssd, w_ssd_out, g_merge, g_merge, bias, bias)


def _out_kernel(m_ref, w_ref, x_ref, fw_ref, o_ref, ssq_scr, *, tn, d_model, final_norm):
    j = pl.program_id(1)

    @pl.when(j == 0)
    def _():
        ssq_scr[...] = jnp.zeros_like(ssq_scr)

    h = x_ref[...] + jnp.dot(m_ref[...], w_ref[...], preferred_element_type=F32)
    o_ref[:, pl.ds(pl.multiple_of(j * tn, tn), tn)] = h
    ssq_scr[...] += jnp.sum(h * h, axis=-1, keepdims=True)

    if final_norm:
        @pl.when(j == pl.num_programs(1) - 1)
        def _():
            inv = lax.rsqrt(ssq_scr[...] * (1.0 / d_model) + NORM_EPS)
            o_ref[...] = o_ref[...] * inv * fw_ref[...]


def _out_proj(merged, w_out, x, final_norm_w, final_norm):
    t, d = x.shape
    tm, tn = _tile(t, 512), _tile(d, 512)
    return pl.pallas_call(
        functools.partial(_out_kernel, tn=tn, d_model=d, final_norm=final_norm),
        grid=(t // tm, d // tn),
        in_specs=[pl.BlockSpec((tm, d), lambda i, j: (i, 0)),
                  pl.BlockSpec((d, tn), lambda i, j: (0, j)),
                  pl.BlockSpec((tm, tn), lambda i, j: (i, j)),
                  pl.BlockSpec((1, d), lambda i, j: (0, 0))],
        out_specs=pl.BlockSpec((tm, d), lambda i, j: (i, 0)),
        out_shape=jax.ShapeDtypeStruct((t, d), F32),
        scratch_shapes=[pltpu.VMEM((tm, 1), F32)],
        compiler_params=_params("parallel", "arbitrary"),
        name="out_proj_norm",
    )(merged, w_out, x, final_norm_w.astype(F32).reshape(1, d))


def kernel(x, norm_w, w_in, conv_w, conv_b, dt_bias, a_log, d_skip, ssd_norm_w, w_attn_out, w_ssd_out,
           gate_bias, w_out, final_norm_w):
    bsz, seq, d_model = x.shape
    depth = norm_w.shape[0]
    attn_w = w_attn_out.shape[1]
    d_inner = w_ssd_out.shape[1]
    n_ssd_heads = a_log.shape[1]
    conv_dim = conv_w.shape[2]
    n_groups = (conv_dim - d_inner) // (2 * SSD_STATE)
    n_attn_heads = attn_w // ATTN_HEAD_DIM
    assert seq % MOBA_BLOCK == 0 and seq % SSD_CHUNK == 0
    c_zx, c_dt = 4 * attn_w, 4 * attn_w + d_inner + conv_dim
    c_gm = c_dt + n_ssd_heads

    h = x.reshape(bsz * seq, d_model)
    for l in range(depth):
        w = w_in[l].astype(BF16)
        u = _rmsnorm(h, norm_w[l], BF16)
        qkvg = _matmul(u, w, BF16, "in_proj_attn", 0, c_zx)
        zx = _matmul(u, w, BF16, "in_proj_ssd", c_zx, c_dt - c_zx)
        dt_raw = _matmul(u, w, F32, "in_proj_dt", c_dt, LANES)
        g_merge = _matmul(u, w[:, c_gm:], BF16, "in_proj_gate")

        qk_rot, kmean = _rotary(qkvg, seq, n_attn_heads)
        o_attn = _attention(qk_rot, qkvg, kmean, bsz, seq, n_attn_heads)
        y_ssd = _ssd(zx, dt_raw, conv_w[l], conv_b[l], dt_bias[l], a_log[l], d_skip[l], ssd_norm_w[l],
                     bsz, seq, d_inner, n_ssd_heads, n_groups)
        merged = _merge(o_attn, w_attn_out[l].astype(BF16), y_ssd, w_ssd_out[l].astype(BF16), g_merge,
                        gate_bias[l])
        h = _out_proj(merged, w_out[l].astype(BF16), h, final_norm_w, final_norm=(l == depth - 1))
    return h.reshape(bsz, seq, d_model)
```

```python
import functools
import math

import jax
import jax.numpy as jnp
from jax import lax
from jax.experimental import pallas as pl
from jax.experimental.pallas import tpu as pltpu

F32 = jnp.float32
BF16 = jnp.bfloat16

NORM_EPS = 1e-6
ATTN_HEAD_DIM = 128
ROT_DIM = ATTN_HEAD_DIM // 4
ROPE_THETA = 500000.0
MOBA_BLOCK = 256
MOBA_TOPK = 3
SSD_HEAD_DIM = 64
SSD_STATE = 128
SSD_CONV = 4
SSD_CHUNK = 256
CONV_CARRY_ROWS = 8
LOG2_E = 1.4426950408889634
VT_ROWS = ATTN_HEAD_DIM + 16

V7X_VMEM_LIMIT_BYTES = 56 * 1024 * 1024
LANES = 128

NT_DIMS = (((1,), (1,)), ((), ()))
TN_DIMS = (((0,), (0,)), ((), ()))


def _params(*semantics):
    return pltpu.CompilerParams(dimension_semantics=semantics,
                                vmem_limit_bytes=V7X_VMEM_LIMIT_BYTES)


def _tile(n, pref):
    for t in range(min(n, pref), 0, -LANES):
        if n % t == 0:
            return t
    raise ValueError(f"no lane-aligned tile for {n}")


def _rmsnorm_kernel(x_ref, w_ref, o_ref):
    x = x_ref[...]
    ms = jnp.mean(x * x, axis=-1, keepdims=True)
    o_ref[...] = (x * lax.rsqrt(ms + NORM_EPS) * w_ref[...]).astype(o_ref.dtype)


def _rmsnorm(x, w, out_dtype):
    t, d = x.shape
    tm = _tile(t, 256)
    return pl.pallas_call(
        _rmsnorm_kernel,
        grid=(t // tm,),
        in_specs=[pl.BlockSpec((tm, d), lambda i: (i, 0)),
                  pl.BlockSpec((1, d), lambda i: (0, 0))],
        out_specs=pl.BlockSpec((tm, d), lambda i: (i, 0)),
        out_shape=jax.ShapeDtypeStruct((t, d), out_dtype),
        compiler_params=_params("parallel"),
        name="rmsnorm",
    )(x, w.reshape(1, d))


def _matmul_kernel(x_ref, w_ref, o_ref):
    o_ref[...] = jnp.dot(x_ref[...], w_ref[...], preferred_element_type=F32).astype(o_ref.dtype)


def _matmul(x, w, out_dtype, name, col0=0, n=None):
    m, k = x.shape
    n = w.shape[1] - col0 if n is None else n
    tm, tn = _tile(m, 1024), _tile(math.gcd(n, col0), 1024)
    j0 = col0 // tn
    return pl.pallas_call(
        _matmul_kernel,
        grid=(m // tm, n // tn),
        in_specs=[pl.BlockSpec((tm, k), lambda i, j: (i, 0)),
                  pl.BlockSpec((k, tn), lambda i, j: (0, j0 + j))],
        out_specs=pl.BlockSpec((tm, tn), lambda i, j: (i, j)),
        out_shape=jax.ShapeDtypeStruct((m, n), out_dtype),
        compiler_params=_params("parallel", "arbitrary"),
        name=name,
    )(x, w)


def _rotary_kernel(qk_ref, v_ref, cos_ref, sin_lo_ref, sin_hi_ref, o_ref, kmean_ref, vt_ref, *, n_heads, q_scale):
    cos, sin_lo, sin_hi = cos_ref[...], sin_lo_ref[...], sin_hi_ref[...]
    half = ROT_DIM // 2
    hd = ATTN_HEAD_DIM
    pad_rows = lax.broadcasted_iota(jnp.int32, (VT_ROWS - hd, MOBA_BLOCK), 0)
    ones_row = jnp.where(pad_rows == 0, 1.0, 0.0).astype(vt_ref.dtype)

    def rot(col):
        t = qk_ref[:, pl.ds(col, LANES)].astype(F32)
        return (t * cos + pltpu.roll(t, LANES - half, 1) * sin_lo + pltpu.roll(t, half, 1) * sin_hi)

    def head(h, carry):
        qcol = pl.multiple_of(h * LANES, LANES)
        kcol = pl.multiple_of((n_heads + h) * LANES, LANES)
        o_ref[:, pl.ds(qcol, LANES)] = (rot(qcol) * q_scale).astype(o_ref.dtype)
        k = rot(kcol)
        o_ref[:, pl.ds(kcol, LANES)] = k.astype(o_ref.dtype)
        kmean_ref[0, :, pl.ds(qcol, LANES)] = jnp.mean(k, axis=0, keepdims=True)
        vt_ref[0, h, 0:hd, :] = v_ref[:, pl.ds(qcol, LANES)].astype(F32).T.astype(vt_ref.dtype)
        vt_ref[0, h, hd:VT_ROWS, :] = ones_row
        return carry

    lax.fori_loop(0, n_heads, head, 0)


def _rotary(qkvg, bsz, seq, n_heads):
    t = qkvg.shape[0]
    width = n_heads * ATTN_HEAD_DIM
    nq = seq // MOBA_BLOCK
    half = ROT_DIM // 2
    inv_freq = ROPE_THETA ** (-jnp.arange(half, dtype=F32) * 2.0 / ROT_DIM)
    ang = jnp.arange(seq).astype(F32)[:, None] * inv_freq[None, :]
    cos, sin = jnp.cos(ang), jnp.sin(ang)
    zeros = lambda n: jnp.zeros((seq, n), F32)
    cos_t = jnp.concatenate([cos, cos, jnp.ones((seq, LANES - ROT_DIM), F32)], axis=1)
    sin_lo = jnp.concatenate([-sin, zeros(LANES - half)], axis=1)
    sin_hi = jnp.concatenate([zeros(half), sin, zeros(LANES - ROT_DIM)], axis=1)
    table_spec = pl.BlockSpec((MOBA_BLOCK, LANES), lambda r: (r % nq, 0))
    return pl.pallas_call(
        functools.partial(_rotary_kernel, n_heads=n_heads, q_scale=ATTN_HEAD_DIM ** -0.5 * LOG2_E),
        grid=(t // MOBA_BLOCK,),
        in_specs=[pl.BlockSpec((MOBA_BLOCK, 2 * width), lambda r: (r, 0)),
                  pl.BlockSpec((MOBA_BLOCK, width), lambda r: (r, 2)),
                  table_spec, table_spec, table_spec],
        out_specs=[pl.BlockSpec((MOBA_BLOCK, 2 * width), lambda r: (r, 0)),
                   pl.BlockSpec((1, 1, width), lambda r: (r, 0, 0)),
                   pl.BlockSpec((1, n_heads, VT_ROWS, MOBA_BLOCK), lambda r: (r // nq, 0, 0, r % nq))],
        out_shape=[jax.ShapeDtypeStruct((t, 2 * width), BF16),
                   jax.ShapeDtypeStruct((t // MOBA_BLOCK, 1, width), F32),
                   jax.ShapeDtypeStruct((bsz, n_heads, VT_ROWS, seq), BF16)],
        compiler_params=_params("parallel"),
        name="rotary_kmean",
    )(qkvg, qkvg, cos_t, sin_lo, sin_hi)


def _attn_kernel(q_ref, k_ref, vt_ref, g_ref, kmean_ref, o_ref, *scratch, nq, n_chains):
    sel_scr = scratch[0]
    s_even, s_odd = scratch[1:1 + n_chains], scratch[1 + n_chains:1 + 2 * n_chains]
    acc_scr = scratch[1 + 2 * n_chains:]
    i = pl.program_id(2)
    blk = MOBA_BLOCK
    hd = ATTN_HEAD_DIM
    neg_inf = -jnp.inf
    lead = (i + 1) % 2
    n_pairs = (i + 1 + lead) // 2
    heads = [slice(c * hd, (c + 1) * hd) for c in range(n_chains)]
    qs = [q_ref[:, hs] for hs in heads]

    def issue_scores(pos, dst):
        j = jnp.maximum(pos - lead, 0)
        rows = pl.ds(pl.multiple_of(j * blk, blk), blk)
        col_max = []
        for c in range(n_chains):
            s = lax.dot_general(k_ref[rows, heads[c]], qs[c], NT_DIMS, preferred_element_type=F32)
            dst[c][...] = s
            col_max.append(jnp.max(s, axis=0, keepdims=True))
        return col_max

    def consume(pos, src, col_max, m_run, own):
        j = jnp.maximum(pos - lead, 0)
        cols = pl.ds(pl.multiple_of(j * blk, blk), blk)
        sel_row = jnp.where(pos < lead, nq, j)
        m_out, staged = [], []
        for c in range(n_chains):
            s = src[c][...]
            if own:
                key_pos = lax.broadcasted_iota(jnp.int32, (blk, blk), 0)
                qry_pos = lax.broadcasted_iota(jnp.int32, (blk, blk), 1)
                s = jnp.where(key_pos <= qry_pos, s, neg_inf)
                chosen = None
                m_new = jnp.maximum(m_run[c], jnp.max(s, axis=0, keepdims=True))
                m_sub = m_new
                alpha = jnp.exp2(m_run[c] - m_new)
            else:
                chosen = sel_scr[c, pl.ds(sel_row, 1), :] > 0.0
                m_new = jnp.maximum(m_run[c], jnp.where(chosen, col_max[c], neg_inf))
                seen = m_new > neg_inf
                m_sub = jnp.where(seen, m_new, 0.0)
                alpha = jnp.where(seen, jnp.exp2(m_run[c] - m_new), 0.0)
            p = jnp.exp2(s - m_sub).astype(BF16)
            m_out.append(m_new)
            staged.append((alpha, p, chosen))
        for c in range(n_chains):
            alpha, p, chosen = staged[c]
            pv = jnp.dot(vt_ref[0, c, :, cols], p, preferred_element_type=F32)
            if chosen is not None:
                pv = jnp.where(chosen, pv, 0.0)
            acc_scr[c][...] = alpha * acc_scr[c][...] + pv
        return m_out

    gates = []
    for c in range(n_chains):
        gates.append(lax.dot_general(kmean_ref[0, :, heads[c]], qs[c].astype(F32), NT_DIMS,
                                     preferred_element_type=F32))
    cm0 = issue_scores(0, s_even)
    n_idx = lax.broadcasted_iota(jnp.int32, (nq, blk), 0)
    past = n_idx < i
    for c in range(n_chains):
        work = jnp.where(past, gates[c], neg_inf)
        chosen = jnp.zeros((nq, blk), jnp.bool_)
        for _ in range(MOBA_TOPK):
            top = jnp.max(work, axis=0, keepdims=True)
            first = jnp.min(jnp.where(work == top, n_idx, nq), axis=0, keepdims=True)
            hit = n_idx == first
            chosen = chosen | hit
            work = jnp.where(hit, neg_inf, work)
        sel_scr[c, 0:nq, :] = jnp.where(chosen & past, 1.0, 0.0)
        sel_scr[c, nq:nq + 8, :] = jnp.zeros((8, blk), F32)
        acc_scr[c][...] = jnp.zeros_like(acc_scr[c])

    m0 = [jnp.full((1, blk), neg_inf, F32) for _ in range(n_chains)]

    def pair(t, carry):
        m_run, cm_even = carry
        cm_odd = issue_scores(2 * t + 1, s_odd)
        m_run = consume(2 * t, s_even, cm_even, m_run, own=False)
        cm_even = issue_scores(2 * t + 2, s_even)
        m_run = consume(2 * t + 1, s_odd, cm_odd, m_run, own=False)
        return m_run, cm_even

    m_run, cm_even = lax.fori_loop(0, n_pairs - 1, pair, (m0, cm0))
    last = 2 * (n_pairs - 1)
    issue_scores(last + 1, s_odd)
    m_run = consume(last, s_even, cm_even, m_run, own=False)
    consume(last + 1, s_odd, None, m_run, own=True)

    for c in range(n_chains):
        acc = acc_scr[c][...]
        out = (acc[0:hd, :] / acc[hd:hd + 1, :]).T
        g = g_ref[:, heads[c]].astype(F32)
        o_ref[:, heads[c]] = (out * (g * jax.nn.sigmoid(g))).astype(o_ref.dtype)


ATTN_HEADS_PER_STEP = 4


def _attention(qk_rot, qkvg, kmean, vt, bsz, seq, n_heads):
    t = qk_rot.shape[0]
    nq = seq // MOBA_BLOCK
    width = n_heads * ATTN_HEAD_DIM
    hp = ATTN_HEADS_PER_STEP if n_heads % ATTN_HEADS_PER_STEP == 0 else 1
    n_hg, cw = n_heads // hp, hp * ATTN_HEAD_DIM
    kmean = kmean.reshape(bsz, nq, width)
    score_buf = pltpu.VMEM((MOBA_BLOCK, MOBA_BLOCK), F32)
    return pl.pallas_call(
        functools.partial(_attn_kernel, nq=nq, n_chains=hp),
        grid=(bsz, n_hg, nq),
        in_specs=[pl.BlockSpec((MOBA_BLOCK, cw), lambda b, h, i: (b * nq + i, h)),
                  pl.BlockSpec((seq, cw), lambda b, h, i: (b, n_hg + h)),
                  pl.BlockSpec((1, hp, VT_ROWS, seq), lambda b, h, i: (b, h, 0, 0)),
                  pl.BlockSpec((MOBA_BLOCK, cw), lambda b, h, i: (b * nq + i, 3 * n_hg + h)),
                  pl.BlockSpec((1, nq, cw), lambda b, h, i: (b, 0, h))],
        out_specs=pl.BlockSpec((MOBA_BLOCK, cw), lambda b, h, i: (b * nq + i, h)),
        out_shape=jax.ShapeDtypeStruct((t, width), BF16),
        scratch_shapes=([pltpu.VMEM((hp, nq + 8, MOBA_BLOCK), F32)]
                        + [score_buf] * (2 * hp)
                        + [pltpu.VMEM((VT_ROWS, MOBA_BLOCK), F32)] * hp),
        compiler_params=_params("parallel", "parallel", "arbitrary"),
        name="moba_attention",
    )(qk_rot, qk_rot, vt, qkvg, kmean)


def _split3_bf16(a):
    hi = a.astype(BF16)
    r = a - hi.astype(F32)
    mid = r.astype(BF16)
    lo = (r - mid.astype(F32)).astype(BF16)
    return hi, mid, lo


def _ssd_kernel(z_ref, xs_ref, bc_ref, dt_ref, convw_ref, convb_ref, dtb_ref, alog_ref, dskip_ref, normw_ref,
                o_ref, carry_scr, ext_scr, xc_scr, bcc_scr, y_scr, state_scr, *, n_groups, heads_per_group):
    c = pl.program_id(1)
    L = SSD_CHUNK
    d_inner = xs_ref.shape[1]
    gw = heads_per_group * SSD_HEAD_DIM
    strip = 512

    @pl.when(c == 0)
    def _():
        carry_scr[...] = jnp.zeros_like(carry_scr)
        state_scr[...] = jnp.zeros_like(state_scr)

    def conv_strip(src_ref, src_col, conv_col, dst_ref, dst_col, width):
        cur = src_ref[:, pl.ds(src_col, width)].astype(F32)
        ext_scr[0:CONV_CARRY_ROWS, 0:width] = carry_scr[:, pl.ds(conv_col, width)]
        ext_scr[CONV_CARRY_ROWS:CONV_CARRY_ROWS + L, 0:width] = cur
        carry_scr[:, pl.ds(conv_col, width)] = cur[L - CONV_CARRY_ROWS:, :]
        acc = jnp.broadcast_to(convb_ref[:, pl.ds(conv_col, width)], (L, width))
        for tap in range(SSD_CONV):
            off = CONV_CARRY_ROWS - (SSD_CONV - 1) + tap
            acc = acc + ext_scr[off:off + L, 0:width] * convw_ref[tap:tap + 1, pl.ds(conv_col, width)]
        dst_ref[:, pl.ds(dst_col, width)] = (acc * jax.nn.sigmoid(acc)).astype(dst_ref.dtype)

    for s in range(d_inner // strip):
        conv_strip(xs_ref, s * strip, s * strip, xc_scr, s * strip, strip)
    for s in range(bc_ref.shape[1] // strip):
        conv_strip(bc_ref, s * strip, d_inner + s * strip, bcc_scr, s * strip, strip)

    xdt = dt_ref[...] + dtb_ref[...]
    dt = jnp.maximum(xdt, 0.0) + jnp.log1p(jnp.exp(-jnp.abs(xdt)))
    a = dt * (-jnp.exp(alog_ref[...])) * LOG2_E
    t_row = lax.broadcasted_iota(jnp.int32, (L, L), 0)
    t_col = lax.broadcasted_iota(jnp.int32, (L, L), 1)
    causal = t_col <= t_row
    tril = jnp.where(causal, 1.0, 0.0).astype(BF16)
    acum = sum(jnp.dot(tril, part, preferred_element_type=F32) for part in _split3_bf16(a))
    a_tot = acum[L - 1:L, :]
    log_dt = jnp.log2(dt)
    src_t = (acum - log_dt).T
    carry_exp = a_tot - acum + log_dt
    state_decay = jnp.exp2(a_tot)

    lane = lax.broadcasted_iota(jnp.int32, (L, LANES), 1)
    low_half = lane < SSD_HEAD_DIM
    n_bc = n_groups * SSD_STATE

    for g in range(n_groups):
        b_g = bcc_scr[:, g * SSD_STATE:(g + 1) * SSD_STATE]
        c_g = bcc_scr[:, n_bc + g * SSD_STATE:n_bc + (g + 1) * SSD_STATE]
        cb = lax.dot_general(c_g, b_g, NT_DIMS, preferred_element_type=F32)
        state_g = state_scr[g]
        y_off = jnp.dot(c_g, state_g.astype(BF16), preferred_element_type=F32)
        for pair in range(heads_per_group // 2):
            col0 = g * gw + pair * LANES
            x_pair = xc_scr[:, col0:col0 + LANES]
            x_lo = jnp.where(low_half, x_pair, 0.0).astype(BF16)
            x_hi = jnp.where(low_half, 0.0, x_pair).astype(BF16)
            h0 = g * heads_per_group + 2 * pair
            y_diag = jnp.zeros((L, LANES), F32)
            for h, x_half in ((h0, x_lo), (h0 + 1, x_hi)):
                seg = acum[:, h:h + 1] - src_t[h:h + 1, :]
                w = (cb * jnp.exp2(jnp.where(causal, seg, -jnp.inf))).astype(BF16)
                y_diag = y_diag + jnp.dot(w, x_half, preferred_element_type=F32)
            in_decay = jnp.where(low_half, jnp.exp2(acum[:, h0:h0 + 1]), jnp.exp2(acum[:, h0 + 1:h0 + 2]))
            y = (y_diag + in_decay * y_off[:, pair * LANES:(pair + 1) * LANES]
                 + dskip_ref[:, col0:col0 + LANES] * x_pair)
            zg = z_ref[:, col0:col0 + LANES].astype(F32)
            y_scr[:, col0:col0 + LANES] = y * (zg * jax.nn.sigmoid(zg))
            to_end = jnp.where(low_half, jnp.exp2(carry_exp[:, h0:h0 + 1]), jnp.exp2(carry_exp[:, h0 + 1:h0 + 2]))
            xc_scr[:, col0:col0 + LANES] = x_pair * to_end
            sd = jnp.where(low_half[0:1, :], state_decay[:, h0:h0 + 1], state_decay[:, h0 + 1:h0 + 2])
            state_scr[g, :, pair * LANES:(pair + 1) * LANES] = state_g[:, pair * LANES:(pair + 1) * LANES] * sd
        x_end = xc_scr[:, g * gw:(g + 1) * gw].astype(BF16)
        state_scr[g] = state_scr[g] + lax.dot_general(b_g, x_end, TN_DIMS, preferred_element_type=F32)
        yg = y_scr[:, g * gw:(g + 1) * gw]
        ms = jnp.mean(yg * yg, axis=-1, keepdims=True)
        o_ref[:, g * gw:(g + 1) * gw] = (yg * lax.rsqrt(ms + NORM_EPS)
                                         * normw_ref[:, g * gw:(g + 1) * gw]).astype(o_ref.dtype)


def _ssd(zx, dt_raw, conv_w, conv_b, dt_bias, a_log, d_skip, norm_w, bsz, seq, d_inner, n_heads, n_groups):
    t = zx.shape[0]
    nc = seq // SSD_CHUNK
    bc_w = 2 * n_groups * SSD_STATE
    conv_dim = d_inner + bc_w
    heads_per_group = n_heads // n_groups
    gw = heads_per_group * SSD_HEAD_DIM
    assert heads_per_group % 2 == 0 and d_inner % 512 == 0 and bc_w % 512 == 0 and d_inner % bc_w == 0
    assert n_heads <= LANES
    pad_heads = lambda v: jnp.pad(v.astype(F32), (0, LANES - n_heads)).reshape(1, LANES)
    row = lambda b, c: (b * nc + c, 0)
    const = lambda b, c: (0, 0)
    return pl.pallas_call(
        functools.partial(_ssd_kernel, n_groups=n_groups, heads_per_group=heads_per_group),
        grid=(bsz, nc),
        in_specs=[pl.BlockSpec((SSD_CHUNK, d_inner), row),
                  pl.BlockSpec((SSD_CHUNK, d_inner), lambda b, c: (b * nc + c, 1)),
                  pl.BlockSpec((SSD_CHUNK, bc_w), lambda b, c: (b * nc + c, 2 * d_inner // bc_w)),
                  pl.BlockSpec((SSD_CHUNK, LANES), row),
                  pl.BlockSpec((SSD_CONV, conv_dim), const),
                  pl.BlockSpec((1, conv_dim), const),
                  pl.BlockSpec((1, LANES), const),
                  pl.BlockSpec((1, LANES), const),
                  pl.BlockSpec((1, d_inner), const),
                  pl.BlockSpec((1, d_inner), const)],
        out_specs=pl.BlockSpec((SSD_CHUNK, d_inner), row),
        out_shape=jax.ShapeDtypeStruct((t, d_inner), BF16),
        scratch_shapes=[pltpu.VMEM((CONV_CARRY_ROWS, conv_dim), F32),
                        pltpu.VMEM((CONV_CARRY_ROWS + SSD_CHUNK, 512), F32),
                        pltpu.VMEM((SSD_CHUNK, d_inner), F32),
                        pltpu.VMEM((SSD_CHUNK, bc_w), BF16),
                        pltpu.VMEM((SSD_CHUNK, d_inner), F32),
                        pltpu.VMEM((n_groups, SSD_STATE, gw), F32)],
        compiler_params=_params("parallel", "arbitrary"),
        name="ssd_scan",
    )(zx, zx, zx, dt_raw, conv_w, conv_b.reshape(1, conv_dim), pad_heads(dt_bias), pad_heads(a_log),
      jnp.repeat(d_skip.astype(F32), SSD_HEAD_DIM).reshape(1, d_inner), norm_w.reshape(1, d_inner))


def _merge_kernel(oa_ref, wa_ref, ys_ref, ws_ref, ga_ref, gs_ref, ba_ref, bs_ref, o_ref):
    y_attn = jnp.dot(oa_ref[...], wa_ref[...], preferred_element_type=F32)
    y_ssd = jnp.dot(ys_ref[...], ws_ref[...], preferred_element_type=F32)
    gate_a = jax.nn.sigmoid(ga_ref[...].astype(F32) + ba_ref[...])
    gate_s = jax.nn.sigmoid(gs_ref[...].astype(F32) + bs_ref[...])
    o_ref[...] = (gate_a * y_attn + gate_s * y_ssd).astype(o_ref.dtype)


def _merge(o_attn, w_attn_out, y_ssd, w_ssd_out, g_merge, gate_bias):
    t, wa = o_attn.shape
    ws = y_ssd.shape[1]
    d = w_attn_out.shape[1]
    tm, tn = _tile(t, 1024), _tile(d, 512)
    nj = d // tn
    bias = gate_bias.astype(F32).reshape(1, 2 * d)
    return pl.pallas_call(
        _merge_kernel,
        grid=(t // tm, nj),
        in_specs=[pl.BlockSpec((tm, wa), lambda i, j: (i, 0)),
                  pl.BlockSpec((wa, tn), lambda i, j: (0, j)),
                  pl.BlockSpec((tm, ws), lambda i, j: (i, 0)),
                  pl.BlockSpec((ws, tn), lambda i, j: (0, j)),
                  pl.BlockSpec((tm, tn), lambda i, j: (i, j)),
                  pl.BlockSpec((tm, tn), lambda i, j: (i, nj + j)),
                  pl.BlockSpec((1, tn), lambda i, j: (0, j)),
                  pl.BlockSpec((1, tn), lambda i, j: (0, nj + j))],
        out_specs=pl.BlockSpec((tm, tn), lambda i, j: (i, j)),
        out_shape=jax.ShapeDtypeStruct((t, d), BF16),
        compiler_params=_params("parallel", "arbitrary"),
        name="branch_merge",
    )(o_attn, w_attn_out, y_ssd, w_ssd_out, g_merge, g_merge, bias, bias)


def _out_kernel(m_ref, w_ref, x_ref, fw_ref, o_ref, ssq_scr, *, tn, d_model, final_norm):
    j = pl.program_id(1)

    @pl.when(j == 0)
    def _():
        ssq_scr[...] = jnp.zeros_like(ssq_scr)

    h = x_ref[...] + jnp.dot(m_ref[...], w_ref[...], preferred_element_type=F32)
    o_ref[:, pl.ds(pl.multiple_of(j * tn, tn), tn)] = h
    ssq_scr[...] += jnp.sum(h * h, axis=-1, keepdims=True)

    if final_norm:
        @pl.when(j == pl.num_programs(1) - 1)
        def _():
            inv = lax.rsqrt(ssq_scr[...] * (1.0 / d_model) + NORM_EPS)
            o_ref[...] = o_ref[...] * inv * fw_ref[...]


def _out_proj(merged, w_out, x, final_norm_w, final_norm):
    t, d = x.shape
    tm, tn = _tile(t, 512), _tile(d, 512)
    return pl.pallas_call(
        functools.partial(_out_kernel, tn=tn, d_model=d, final_norm=final_norm),
        grid=(t // tm, d // tn),
        in_specs=[pl.BlockSpec((tm, d), lambda i, j: (i, 0)),
                  pl.BlockSpec((d, tn), lambda i, j: (0, j)),
                  pl.BlockSpec((tm, tn), lambda i, j: (i, j)),
                  pl.BlockSpec((1, d), lambda i, j: (0, 0))],
        out_specs=pl.BlockSpec((tm, d), lambda i, j: (i, 0)),
        out_shape=jax.ShapeDtypeStruct((t, d), F32),
        scratch_shapes=[pltpu.VMEM((tm, 1), F32)],
        compiler_params=_params("parallel", "arbitrary"),
        name="out_proj_norm",
    )(merged, w_out, x, final_norm_w.astype(F32).reshape(1, d))


def kernel(x, norm_w, w_in, conv_w, conv_b, dt_bias, a_log, d_skip, ssd_norm_w, w_attn_out, w_ssd_out,
           gate_bias, w_out, final_norm_w):
    bsz, seq, d_model = x.shape
    depth = norm_w.shape[0]
    attn_w = w_attn_out.shape[1]
    d_inner = w_ssd_out.shape[1]
    n_ssd_heads = a_log.shape[1]
    conv_dim = conv_w.shape[2]
    n_groups = (conv_dim - d_inner) // (2 * SSD_STATE)
    n_attn_heads = attn_w // ATTN_HEAD_DIM
    assert seq % MOBA_BLOCK == 0 and seq % SSD_CHUNK == 0
    c_zx, c_dt = 4 * attn_w, 4 * attn_w + d_inner + conv_dim
    c_gm = c_dt + n_ssd_heads

    h = x.reshape(bsz * seq, d_model)
    for l in range(depth):
        w = w_in[l].astype(BF16)
        u = _rmsnorm(h, norm_w[l], BF16)
        qkvg = _matmul(u, w, BF16, "in_proj_attn", 0, c_zx)
        zx = _matmul(u, w, BF16, "in_proj_ssd", c_zx, c_dt - c_zx)
        dt_raw = _matmul(u, w, F32, "in_proj_dt", c_dt, LANES)
        g_merge = _matmul(u, w[:, c_gm:], BF16, "in_proj_gate")

        qk_rot, kmean, vt = _rotary(qkvg, bsz, seq, n_attn_heads)
        o_attn = _attention(qk_rot, qkvg, kmean, vt, bsz, seq, n_attn_heads)
        y_ssd = _ssd(zx, dt_raw, conv_w[l], conv_b[l], dt_bias[l], a_log[l], d_skip[l], ssd_norm_w[l],
                     bsz, seq, d_inner, n_ssd_heads, n_groups)
        merged = _merge(o_attn, w_attn_out[l].astype(BF16), y_ssd, w_ssd_out[l].astype(BF16), g_merge,
                        gate_bias[l])
        h = _out_proj(merged, w_out[l].astype(BF16), h, final_norm_w, final_norm=(l == depth - 1))
    return h.reshape(bsz, seq, d_model)
```

```python
import functools
import math

import jax
import jax.numpy as jnp
from jax import lax
from jax.experimental import pallas as pl
from jax.experimental.pallas import tpu as pltpu

F32 = jnp.float32
BF16 = jnp.bfloat16

NORM_EPS = 1e-6
ATTN_HEAD_DIM = 128
ROT_DIM = ATTN_HEAD_DIM // 4
ROPE_THETA = 500000.0
MOBA_BLOCK = 256
MOBA_TOPK = 3
SSD_HEAD_DIM = 64
SSD_STATE = 128
SSD_CONV = 4
SSD_CHUNK = 256
CONV_CARRY_ROWS = 8
LOG2_E = 1.4426950408889634
VT_ROWS = ATTN_HEAD_DIM + 16

V7X_VMEM_LIMIT_BYTES = 56 * 1024 * 1024
LANES = 128

NT_DIMS = (((1,), (1,)), ((), ()))
TN_DIMS = (((0,), (0,)), ((), ()))


def _params(*semantics):
    return pltpu.CompilerParams(dimension_semantics=semantics,
                                vmem_limit_bytes=V7X_VMEM_LIMIT_BYTES)


def _tile(n, pref):
    for t in range(min(n, pref), 0, -LANES):
        if n % t == 0:
            return t
    raise ValueError(f"no lane-aligned tile for {n}")


def _rmsnorm_kernel(x_ref, w_ref, o_ref):
    x = x_ref[...]
    ms = jnp.mean(x * x, axis=-1, keepdims=True)
    o_ref[...] = (x * lax.rsqrt(ms + NORM_EPS) * w_ref[...]).astype(o_ref.dtype)


def _rmsnorm(x, w, out_dtype):
    t, d = x.shape
    tm = _tile(t, 256)
    return pl.pallas_call(
        _rmsnorm_kernel,
        grid=(t // tm,),
        in_specs=[pl.BlockSpec((tm, d), lambda i: (i, 0)),
                  pl.BlockSpec((1, d), lambda i: (0, 0))],
        out_specs=pl.BlockSpec((tm, d), lambda i: (i, 0)),
        out_shape=jax.ShapeDtypeStruct((t, d), out_dtype),
        compiler_params=_params("parallel"),
        name="rmsnorm",
    )(x, w.reshape(1, d))


def _matmul_kernel(x_ref, w_ref, o_ref):
    o_ref[...] = jnp.dot(x_ref[...], w_ref[...], preferred_element_type=F32).astype(o_ref.dtype)


def _matmul(x, w, out_dtype, name, col0=0, n=None):
    m, k = x.shape
    n = w.shape[1] - col0 if n is None else n
    tm, tn = _tile(m, 1024), _tile(math.gcd(n, col0), 1024)
    j0 = col0 // tn
    return pl.pallas_call(
        _matmul_kernel,
        grid=(m // tm, n // tn),
        in_specs=[pl.BlockSpec((tm, k), lambda i, j: (i, 0)),
                  pl.BlockSpec((k, tn), lambda i, j: (0, j0 + j))],
        out_specs=pl.BlockSpec((tm, tn), lambda i, j: (i, j)),
        out_shape=jax.ShapeDtypeStruct((m, n), out_dtype),
        compiler_params=_params("parallel", "arbitrary"),
        name=name,
    )(x, w)


def _rotary_kernel(qk_ref, v_ref, cos_ref, sin_lo_ref, sin_hi_ref, o_ref, kmean_ref, vt_ref, *, n_heads, q_scale):
    cos, sin_lo, sin_hi = cos_ref[...], sin_lo_ref[...], sin_hi_ref[...]
    half = ROT_DIM // 2
    hd = ATTN_HEAD_DIM
    pad_rows = lax.broadcasted_iota(jnp.int32, (VT_ROWS - hd, MOBA_BLOCK), 0)
    ones_row = jnp.where(pad_rows == 0, 1.0, 0.0).astype(vt_ref.dtype)

    def rot(col):
        t = qk_ref[:, pl.ds(col, LANES)].astype(F32)
        return (t * cos + pltpu.roll(t, LANES - half, 1) * sin_lo + pltpu.roll(t, half, 1) * sin_hi)

    def head(h, carry):
        qcol = pl.multiple_of(h * LANES, LANES)
        kcol = pl.multiple_of((n_heads + h) * LANES, LANES)
        o_ref[:, pl.ds(qcol, LANES)] = (rot(qcol) * q_scale).astype(o_ref.dtype)
        k = rot(kcol)
        o_ref[:, pl.ds(kcol, LANES)] = k.astype(o_ref.dtype)
        kmean_ref[0, :, pl.ds(qcol, LANES)] = jnp.mean(k, axis=0, keepdims=True)
        vt_ref[0, h, 0:hd, :] = v_ref[:, pl.ds(qcol, LANES)].astype(F32).T.astype(vt_ref.dtype)
        vt_ref[0, h, hd:VT_ROWS, :] = ones_row
        return carry

    lax.fori_loop(0, n_heads, head, 0)


def _rotary(qkvg, bsz, seq, n_heads):
    t = qkvg.shape[0]
    width = n_heads * ATTN_HEAD_DIM
    nq = seq // MOBA_BLOCK
    half = ROT_DIM // 2
    inv_freq = ROPE_THETA ** (-jnp.arange(half, dtype=F32) * 2.0 / ROT_DIM)
    ang = jnp.arange(seq).astype(F32)[:, None] * inv_freq[None, :]
    cos, sin = jnp.cos(ang), jnp.sin(ang)
    zeros = lambda n: jnp.zeros((seq, n), F32)
    cos_t = jnp.concatenate([cos, cos, jnp.ones((seq, LANES - ROT_DIM), F32)], axis=1)
    sin_lo = jnp.concatenate([-sin, zeros(LANES - half)], axis=1)
    sin_hi = jnp.concatenate([zeros(half), sin, zeros(LANES - ROT_DIM)], axis=1)
    table_spec = pl.BlockSpec((MOBA_BLOCK, LANES), lambda r: (r % nq, 0))
    return pl.pallas_call(
        functools.partial(_rotary_kernel, n_heads=n_heads, q_scale=ATTN_HEAD_DIM ** -0.5 * LOG2_E),
        grid=(t // MOBA_BLOCK,),
        in_specs=[pl.BlockSpec((MOBA_BLOCK, 2 * width), lambda r: (r, 0)),
                  pl.BlockSpec((MOBA_BLOCK, width), lambda r: (r, 2)),
                  table_spec, table_spec, table_spec],
        out_specs=[pl.BlockSpec((MOBA_BLOCK, 2 * width), lambda r: (r, 0)),
                   pl.BlockSpec((1, 1, width), lambda r: (r, 0, 0)),
                   pl.BlockSpec((1, n_heads, VT_ROWS, MOBA_BLOCK), lambda r: (r // nq, 0, 0, r % nq))],
        out_shape=[jax.ShapeDtypeStruct((t, 2 * width), BF16),
                   jax.ShapeDtypeStruct((t // MOBA_BLOCK, 1, width), F32),
                   jax.ShapeDtypeStruct((bsz, n_heads, VT_ROWS, seq), BF16)],
        compiler_params=_params("parallel"),
        name="rotary_kmean",
    )(qkvg, qkvg, cos_t, sin_lo, sin_hi)


def _attn_kernel(q_ref, k_ref, vt_ref, g_ref, kmean_ref, o_ref, *scratch, nq, n_chains):
    sel_scr = scratch[0]
    s_even, s_odd = scratch[1:1 + n_chains], scratch[1 + n_chains:1 + 2 * n_chains]
    acc_scr = scratch[1 + 2 * n_chains:]
    i = pl.program_id(2)
    blk = MOBA_BLOCK
    hd = ATTN_HEAD_DIM
    neg_inf = -jnp.inf
    lead = (i + 1) % 2
    n_pairs = (i + 1 + lead) // 2
    heads = [slice(c * hd, (c + 1) * hd) for c in range(n_chains)]
    qs = [q_ref[:, hs] for hs in heads]

    def issue_scores(pos, dst):
        j = jnp.maximum(pos - lead, 0)
        rows = pl.ds(pl.multiple_of(j * blk, blk), blk)
        col_max = []
        for c in range(n_chains):
            s = lax.dot_general(k_ref[rows, heads[c]], qs[c], NT_DIMS, preferred_element_type=F32)
            dst[c][...] = s
            col_max.append(jnp.max(s, axis=0, keepdims=True))
        return col_max

    def consume(pos, src, col_max, m_run, own):
        j = jnp.maximum(pos - lead, 0)
        cols = pl.ds(pl.multiple_of(j * blk, blk), blk)
        sel_row = jnp.where(pos < lead, nq, j)
        m_out, staged = [], []
        for c in range(n_chains):
            s = src[c][...]
            if own:
                key_pos = lax.broadcasted_iota(jnp.int32, (blk, blk), 0)
                qry_pos = lax.broadcasted_iota(jnp.int32, (blk, blk), 1)
                s = jnp.where(key_pos <= qry_pos, s, neg_inf)
                chosen = None
                m_new = jnp.maximum(m_run[c], jnp.max(s, axis=0, keepdims=True))
                m_sub = m_new
                alpha = jnp.exp2(m_run[c] - m_new)
            else:
                chosen = sel_scr[c, pl.ds(sel_row, 1), :] > 0.0
                m_new = jnp.maximum(m_run[c], jnp.where(chosen, col_max[c], neg_inf))
                seen = m_new > neg_inf
                m_sub = jnp.where(seen, m_new, 0.0)
                alpha = jnp.where(seen, jnp.exp2(m_run[c] - m_new), 0.0)
            p = jnp.exp2(s - m_sub).astype(BF16)
            m_out.append(m_new)
            staged.append((alpha, p, chosen))
        for c in range(n_chains):
            alpha, p, chosen = staged[c]
            pv = jnp.dot(vt_ref[0, c, :, cols], p, preferred_element_type=F32)
            if chosen is not None:
                pv = jnp.where(chosen, pv, 0.0)
            acc_scr[c][...] = alpha * acc_scr[c][...] + pv
        return m_out

    gates = []
    for c in range(n_chains):
        gates.append(lax.dot_general(kmean_ref[0, :, heads[c]], qs[c].astype(F32), NT_DIMS,
                                     preferred_element_type=F32))
    cm0 = issue_scores(0, s_even)
    n_idx = lax.broadcasted_iota(jnp.int32, (nq, blk), 0)
    past = n_idx < i
    for c in range(n_chains):
        work = jnp.where(past, gates[c], neg_inf)
        chosen = jnp.zeros((nq, blk), jnp.bool_)
        for _ in range(MOBA_TOPK):
            top = jnp.max(work, axis=0, keepdims=True)
            first = jnp.min(jnp.where(work == top, n_idx, nq), axis=0, keepdims=True)
            hit = n_idx == first
            chosen = chosen | hit
            work = jnp.where(hit, neg_inf, work)
        sel_scr[c, 0:nq, :] = jnp.where(chosen & past, 1.0, 0.0)
        sel_scr[c, nq:nq + 8, :] = jnp.zeros((8, blk), F32)
        acc_scr[c][...] = jnp.zeros_like(acc_scr[c])

    m0 = [jnp.full((1, blk), neg_inf, F32) for _ in range(n_chains)]

    def pair(t, carry):
        m_run, cm_even = carry
        cm_odd = issue_scores(2 * t + 1, s_odd)
        m_run = consume(2 * t, s_even, cm_even, m_run, own=False)
        cm_even = issue_scores(2 * t + 2, s_even)
        m_run = consume(2 * t + 1, s_odd, cm_odd, m_run, own=False)
        return m_run, cm_even

    m_run, cm_even = lax.fori_loop(0, n_pairs - 1, pair, (m0, cm0))
    last = 2 * (n_pairs - 1)
    issue_scores(last + 1, s_odd)
    m_run = consume(last, s_even, cm_even, m_run, own=False)
    consume(last + 1, s_odd, None, m_run, own=True)

    for c in range(n_chains):
        acc = acc_scr[c][...]
        out = (acc[0:hd, :] / acc[hd:hd + 1, :]).T
        g = g_ref[:, heads[c]].astype(F32)
        o_ref[:, heads[c]] = (out * (g * jax.nn.sigmoid(g))).astype(o_ref.dtype)


ATTN_HEADS_PER_STEP = 4


def _attention(qk_rot, qkvg, kmean, vt, bsz, seq, n_heads):
    t = qk_rot.shape[0]
    nq = seq // MOBA_BLOCK
    width = n_heads * ATTN_HEAD_DIM
    hp = ATTN_HEADS_PER_STEP if n_heads % ATTN_HEADS_PER_STEP == 0 else 1
    n_hg, cw = n_heads // hp, hp * ATTN_HEAD_DIM
    kmean = kmean.reshape(bsz, nq, width)
    score_buf = pltpu.VMEM((MOBA_BLOCK, MOBA_BLOCK), F32)
    return pl.pallas_call(
        functools.partial(_attn_kernel, nq=nq, n_chains=hp),
        grid=(bsz, n_hg, nq),
        in_specs=[pl.BlockSpec((MOBA_BLOCK, cw), lambda b, h, i: (b * nq + i, h)),
                  pl.BlockSpec((seq, cw), lambda b, h, i: (b, n_hg + h)),
                  pl.BlockSpec((1, hp, VT_ROWS, seq), lambda b, h, i: (b, h, 0, 0)),
                  pl.BlockSpec((MOBA_BLOCK, cw), lambda b, h, i: (b * nq + i, 3 * n_hg + h)),
                  pl.BlockSpec((1, nq, cw), lambda b, h, i: (b, 0, h))],
        out_specs=pl.BlockSpec((MOBA_BLOCK, cw), lambda b, h, i: (b * nq + i, h)),
        out_shape=jax.ShapeDtypeStruct((t, width), BF16),
        scratch_shapes=([pltpu.VMEM((hp, nq + 8, MOBA_BLOCK), F32)]
                        + [score_buf] * (2 * hp)
                        + [pltpu.VMEM((VT_ROWS, MOBA_BLOCK), F32)] * hp),
        compiler_params=_params("parallel", "parallel", "arbitrary"),
        name="moba_attention",
    )(qk_rot, qk_rot, vt, qkvg, kmean)


def _xbc_proj_kernel(x_ref, w_ref, cw_ref, cb_ref, o_ref, raw_a, raw_b, carry_scr, *, n_tiles, n_steps, tiles_per_seq):
    t = pl.program_id(0)
    tm, tn = x_ref.shape[0], w_ref.shape[1]
    col_chunk, mm_rows, k_chunk = 256, 256, 512
    n_k = x_ref.shape[1] // k_chunk
    piece = mm_rows // n_k

    def step(src, dst):
        if src is not None:
            row_tile, col_tile = (t - 1) // n_tiles, (t - 1) % n_tiles
            prev = carry_scr[col_tile]
            src[0:CONV_CARRY_ROWS, :] = jnp.where(row_tile % tiles_per_seq == 0, jnp.zeros_like(prev), prev)
            carry_scr[col_tile] = src[tm:tm + CONV_CARRY_ROWS, :]
        for c in range(tn // col_chunk):
            cols = slice(c * col_chunk, (c + 1) * col_chunk)
            for rm in range(tm // mm_rows):
                rows = slice(rm * mm_rows, (rm + 1) * mm_rows)
                part = None
                for kc in range(n_k):
                    if dst is not None:
                        ks = slice(kc * k_chunk, (kc + 1) * k_chunk)
                        d = jnp.dot(x_ref[rows, ks], w_ref[ks, cols], preferred_element_type=F32)
                        part = d if part is None else part + d
                    if src is not None:
                        r0 = rm * mm_rows + kc * piece
                        acc = jnp.broadcast_to(cb_ref[:, cols], (piece, col_chunk))
                        for tap in range(SSD_CONV):
                            off = r0 + CONV_CARRY_ROWS - (SSD_CONV - 1) + tap
                            acc = acc + src[off:off + piece, cols] * cw_ref[tap:tap + 1, cols]
                        o_ref[r0:r0 + piece, cols] = (acc * jax.nn.sigmoid(acc)).astype(o_ref.dtype)
                if dst is not None:
                    dst[CONV_CARRY_ROWS + rm * mm_rows:CONV_CARRY_ROWS + (rm + 1) * mm_rows, cols] = part

    @pl.when(t == 0)
    def _():
        step(None, raw_a)

    @pl.when((t % 2 == 1) & (t < n_steps))
    def _():
        step(raw_a, raw_b)

    @pl.when((t % 2 == 0) & (t > 0) & (t < n_steps))
    def _():
        step(raw_b, raw_a)

    @pl.when(t == n_steps)
    def _():
        step(raw_a if (n_steps - 1) % 2 == 0 else raw_b, None)


def _xbc_proj(u, w, col0, conv_w, conv_b, seq):
    m, k = u.shape
    conv_dim = conv_w.shape[1]
    tm, tn = _tile(math.gcd(m, seq), 1024), _tile(math.gcd(conv_dim, col0), 1024)
    n_tiles, j0 = conv_dim // tn, col0 // tn
    n_steps = (m // tm) * n_tiles
    assert tm % 256 == 0 and tn % 256 == 0 and k % 512 == 0
    proj = lambda t: jnp.minimum(t, n_steps - 1)
    conv = lambda t: jnp.maximum(t - 1, 0)
    raw = pltpu.VMEM((CONV_CARRY_ROWS + tm, tn), F32)
    return pl.pallas_call(
        functools.partial(_xbc_proj_kernel, n_tiles=n_tiles, n_steps=n_steps, tiles_per_seq=seq // tm),
        grid=(n_steps + 1,),
        in_specs=[pl.BlockSpec((tm, k), lambda t: (proj(t) // n_tiles, 0)),
                  pl.BlockSpec((k, tn), lambda t: (0, j0 + proj(t) % n_tiles)),
                  pl.BlockSpec((SSD_CONV, tn), lambda t: (0, conv(t) % n_tiles)),
                  pl.BlockSpec((1, tn), lambda t: (0, conv(t) % n_tiles))],
        out_specs=pl.BlockSpec((tm, tn), lambda t: (conv(t) // n_tiles, conv(t) % n_tiles)),
        out_shape=jax.ShapeDtypeStruct((m, conv_dim), BF16),
        scratch_shapes=[raw, raw, pltpu.VMEM((n_tiles, CONV_CARRY_ROWS, tn), F32)],
        compiler_params=_params("arbitrary"),
        name="in_proj_xbc_conv",
    )(u, w, conv_w, conv_b.reshape(1, conv_dim))


def _split3_bf16(a):
    hi = a.astype(BF16)
    r = a - hi.astype(F32)
    mid = r.astype(BF16)
    lo = (r - mid.astype(F32)).astype(BF16)
    return hi, mid, lo


def _ssd_kernel(z_ref, xs_ref, bc_ref, dt_ref, dtb_ref, alog_ref, dskip_ref, normw_ref,
                o_ref, xe_scr, y_scr, state_scr, *, n_groups, heads_per_group):
    c = pl.program_id(1)
    L = SSD_CHUNK
    gw = heads_per_group * SSD_HEAD_DIM

    @pl.when(c == 0)
    def _():
        state_scr[...] = jnp.zeros_like(state_scr)

    xdt = dt_ref[...] + dtb_ref[...]
    dt = jnp.maximum(xdt, 0.0) + jnp.log1p(jnp.exp(-jnp.abs(xdt)))
    a = dt * (-jnp.exp(alog_ref[...])) * LOG2_E
    t_row = lax.broadcasted_iota(jnp.int32, (L, L), 0)
    t_col = lax.broadcasted_iota(jnp.int32, (L, L), 1)
    causal = t_col <= t_row
    tril = jnp.where(causal, 1.0, 0.0).astype(BF16)
    acum = sum(jnp.dot(tril, part, preferred_element_type=F32) for part in _split3_bf16(a))
    a_tot = acum[L - 1:L, :]
    log_dt = jnp.log2(dt)
    src_t = (acum - log_dt).T
    carry_exp = a_tot - acum + log_dt
    state_decay = jnp.exp2(a_tot)

    lane = lax.broadcasted_iota(jnp.int32, (L, LANES), 1)
    low_half = lane < SSD_HEAD_DIM
    n_bc = n_groups * SSD_STATE

    for g in range(n_groups):
        b_g = bc_ref[:, g * SSD_STATE:(g + 1) * SSD_STATE]
        c_g = bc_ref[:, n_bc + g * SSD_STATE:n_bc + (g + 1) * SSD_STATE]
        cb = lax.dot_general(c_g, b_g, NT_DIMS, preferred_element_type=F32)
        state_g = state_scr[g]
        y_off = jnp.dot(c_g, state_g.astype(BF16), preferred_element_type=F32)
        for pair in range(heads_per_group // 2):
            col0 = g * gw + pair * LANES
            x_pair = xs_ref[:, col0:col0 + LANES].astype(F32)
            x_lo = jnp.where(low_half, x_pair, 0.0).astype(BF16)
            x_hi = jnp.where(low_half, 0.0, x_pair).astype(BF16)
            h0 = g * heads_per_group + 2 * pair
            y_diag = jnp.zeros((L, LANES), F32)
            head_decay = []
            for h, x_half in ((h0, x_lo), (h0 + 1, x_hi)):
                acum_h = jnp.broadcast_to(acum[:, h:h + 1], (L, LANES))
                seg = jnp.concatenate([acum_h] * (L // LANES), axis=1) - src_t[h:h + 1, :]
                w = (cb * jnp.exp2(jnp.where(causal, seg, -jnp.inf))).astype(BF16)
                y_diag = y_diag + jnp.dot(w, x_half, preferred_element_type=F32)
                head_decay.append(jnp.exp2(acum_h))
            in_decay = jnp.where(low_half, head_decay[0], head_decay[1])
            y = (y_diag + in_decay * y_off[:, pair * LANES:(pair + 1) * LANES]
                 + dskip_ref[:, col0:col0 + LANES] * x_pair)
            zg = z_ref[:, col0:col0 + LANES].astype(F32)
            y_scr[:, col0:col0 + LANES] = y * (zg * jax.nn.sigmoid(zg))
            to_end = jnp.where(low_half, jnp.exp2(carry_exp[:, h0:h0 + 1]), jnp.exp2(carry_exp[:, h0 + 1:h0 + 2]))
            xe_scr[g, :, pair * LANES:(pair + 1) * LANES] = (x_pair * to_end).astype(xe_scr.dtype)
            sd = jnp.where(low_half[0:1, :], state_decay[:, h0:h0 + 1], state_decay[:, h0 + 1:h0 + 2])
            state_scr[g, :, pair * LANES:(pair + 1) * LANES] = state_g[:, pair * LANES:(pair + 1) * LANES] * sd
        state_scr[g] = state_scr[g] + lax.dot_general(b_g, xe_scr[g], TN_DIMS, preferred_element_type=F32)
        yg = y_scr[:, g * gw:(g + 1) * gw]
        ms = jnp.mean(yg * yg, axis=-1, keepdims=True)
        o_ref[:, g * gw:(g + 1) * gw] = (yg * lax.rsqrt(ms + NORM_EPS)
                                         * normw_ref[:, g * gw:(g + 1) * gw]).astype(o_ref.dtype)


def _ssd(z, xbc, dt_raw, dt_bias, a_log, d_skip, norm_w, bsz, seq, d_inner, n_heads, n_groups):
    t = z.shape[0]
    nc = seq // SSD_CHUNK
    bc_w = 2 * n_groups * SSD_STATE
    heads_per_group = n_heads // n_groups
    gw = heads_per_group * SSD_HEAD_DIM
    assert heads_per_group % 2 == 0 and d_inner % bc_w == 0 and n_heads <= LANES
    pad_heads = lambda v: jnp.pad(v.astype(F32), (0, LANES - n_heads)).reshape(1, LANES)
    row = lambda b, c: (b * nc + c, 0)
    const = lambda b, c: (0, 0)
    return pl.pallas_call(
        functools.partial(_ssd_kernel, n_groups=n_groups, heads_per_group=heads_per_group),
        grid=(bsz, nc),
        in_specs=[pl.BlockSpec((SSD_CHUNK, d_inner), row),
                  pl.BlockSpec((SSD_CHUNK, d_inner), row),
                  pl.BlockSpec((SSD_CHUNK, bc_w), lambda b, c: (b * nc + c, d_inner // bc_w)),
                  pl.BlockSpec((SSD_CHUNK, LANES), row),
                  pl.BlockSpec((1, LANES), const),
                  pl.BlockSpec((1, LANES), const),
                  pl.BlockSpec((1, d_inner), const),
                  pl.BlockSpec((1, d_inner), const)],
        out_specs=pl.BlockSpec((SSD_CHUNK, d_inner), row),
        out_shape=jax.ShapeDtypeStruct((t, d_inner), BF16),
        scratch_shapes=[pltpu.VMEM((n_groups, SSD_CHUNK, gw), BF16),
                        pltpu.VMEM((SSD_CHUNK, d_inner), F32),
                        pltpu.VMEM((n_groups, SSD_STATE, gw), F32)],
        compiler_params=_params("parallel", "arbitrary"),
        name="ssd_scan",
    )(z, xbc, xbc, dt_raw, pad_heads(dt_bias), pad_heads(a_log),
      jnp.repeat(d_skip.astype(F32), SSD_HEAD_DIM).reshape(1, d_inner), norm_w.reshape(1, d_inner))


def _merge_kernel(oa_ref, wa_ref, ys_ref, ws_ref, ga_ref, gs_ref, ba_ref, bs_ref, o_ref):
    y_attn = jnp.dot(oa_ref[...], wa_ref[...], preferred_element_type=F32)
    y_ssd = jnp.dot(ys_ref[...], ws_ref[...], preferred_element_type=F32)
    gate_a = jax.nn.sigmoid(ga_ref[...].astype(F32) + ba_ref[...])
    gate_s = jax.nn.sigmoid(gs_ref[...].astype(F32) + bs_ref[...])
    o_ref[...] = (gate_a * y_attn + gate_s * y_ssd).astype(o_ref.dtype)


def _merge(o_attn, w_attn_out, y_ssd, w_ssd_out, g_merge, gate_bias):
    t, wa = o_attn.shape
    ws = y_ssd.shape[1]
    d = w_attn_out.shape[1]
    tm, tn = _tile(t, 1024), _tile(d, 512)
    nj = d // tn
    bias = gate_bias.astype(F32).reshape(1, 2 * d)
    return pl.pallas_call(
        _merge_kernel,
        grid=(t // tm, nj),
        in_specs=[pl.BlockSpec((tm, wa), lambda i, j: (i, 0)),
                  pl.BlockSpec((wa, tn), lambda i, j: (0, j)),
                  pl.BlockSpec((tm, ws), lambda i, j: (i, 0)),
                  pl.BlockSpec((ws, tn), lambda i, j: (0, j)),
                  pl.BlockSpec((tm, tn), lambda i, j: (i, j)),
                  pl.BlockSpec((tm, tn), lambda i, j: (i, nj + j)),
                  pl.BlockSpec((1, tn), lambda i, j: (0, j)),
                  pl.BlockSpec((1, tn), lambda i, j: (0, nj + j))],
        out_specs=pl.BlockSpec((tm, tn), lambda i, j: (i, j)),
        out_shape=jax.ShapeDtypeStruct((t, d), BF16),
        compiler_params=_params("parallel", "arbitrary"),
        name="branch_merge",
    )(o_attn, w_attn_out, y_ssd, w_ssd_out, g_merge, g_merge, bias, bias)


def _out_kernel(m_ref, w_ref, x_ref, fw_ref, o_ref, ssq_scr, *, tn, d_model, final_norm):
    j = pl.program_id(1)

    @pl.when(j == 0)
    def _():
        ssq_scr[...] = jnp.zeros_like(ssq_scr)

    h = x_ref[...] + jnp.dot(m_ref[...], w_ref[...], preferred_element_type=F32)
    o_ref[:, pl.ds(pl.multiple_of(j * tn, tn), tn)] = h
    ssq_scr[...] += jnp.sum(h * h, axis=-1, keepdims=True)

    if final_norm:
        @pl.when(j == pl.num_programs(1) - 1)
        def _():
            inv = lax.rsqrt(ssq_scr[...] * (1.0 / d_model) + NORM_EPS)
            o_ref[...] = o_ref[...] * inv * fw_ref[...]


def _out_proj(merged, w_out, x, final_norm_w, final_norm):
    t, d = x.shape
    tm, tn = _tile(t, 512), _tile(d, 512)
    return pl.pallas_call(
        functools.partial(_out_kernel, tn=tn, d_model=d, final_norm=final_norm),
        grid=(t // tm, d // tn),
        in_specs=[pl.BlockSpec((tm, d), lambda i, j: (i, 0)),
                  pl.BlockSpec((d, tn), lambda i, j: (0, j)),
                  pl.BlockSpec((tm, tn), lambda i, j: (i, j)),
                  pl.BlockSpec((1, d), lambda i, j: (0, 0))],
        out_specs=pl.BlockSpec((tm, d), lambda i, j: (i, 0)),
        out_shape=jax.ShapeDtypeStruct((t, d), F32),
        scratch_shapes=[pltpu.VMEM((tm, 1), F32)],
        compiler_params=_params("parallel", "arbitrary"),
        name="out_proj_norm",
    )(merged, w_out, x, final_norm_w.astype(F32).reshape(1, d))


def kernel(x, norm_w, w_in, conv_w, conv_b, dt_bias, a_log, d_skip, ssd_norm_w, w_attn_out, w_ssd_out,
           gate_bias, w_out, final_norm_w):
    bsz, seq, d_model = x.shape
    depth = norm_w.shape[0]
    attn_w = w_attn_out.shape[1]
    d_inner = w_ssd_out.shape[1]
    n_ssd_heads = a_log.shape[1]
    conv_dim = conv_w.shape[2]
    n_groups = (conv_dim - d_inner) // (2 * SSD_STATE)
    n_attn_heads = attn_w // ATTN_HEAD_DIM
    assert seq % MOBA_BLOCK == 0 and seq % SSD_CHUNK == 0
    c_zx, c_dt = 4 * attn_w, 4 * attn_w + d_inner + conv_dim
    c_gm = c_dt + n_ssd_heads

    h = x.reshape(bsz * seq, d_model)
    for l in range(depth):
        w = w_in[l].astype(BF16)
        u = _rmsnorm(h, norm_w[l], BF16)
        qkvg = _matmul(u, w, BF16, "in_proj_attn", 0, c_zx)
        z = _matmul(u, w, BF16, "in_proj_z", c_zx, d_inner)
        xbc = _xbc_proj(u, w, c_zx + d_inner, conv_w[l], conv_b[l], seq)
        dt_raw = _matmul(u, w, F32, "in_proj_dt", c_dt, LANES)
        g_merge = _matmul(u, w[:, c_gm:], BF16, "in_proj_gate")

        qk_rot, kmean, vt = _rotary(qkvg, bsz, seq, n_attn_heads)
        o_attn = _attention(qk_rot, qkvg, kmean, vt, bsz, seq, n_attn_heads)
        y_ssd = _ssd(z, xbc, dt_raw, dt_bias[l], a_log[l], d_skip[l], ssd_norm_w[l],
                     bsz, seq, d_inner, n_ssd_heads, n_groups)
        merged = _merge(o_attn, w_attn_out[l].astype(BF16), y_ssd, w_ssd_out[l].astype(BF16), g_merge,
                        gate_bias[l])
        h = _out_proj(merged, w_out[l].astype(BF16), h, final_norm_w, final_norm=(l == depth - 1))
    return h.reshape(bsz, seq, d_model)
```

```python
import functools
import math

import jax
import jax.numpy as jnp
from jax import lax
from jax.experimental import pallas as pl
from jax.experimental.pallas import tpu as pltpu

F32 = jnp.float32
BF16 = jnp.bfloat16

NORM_EPS = 1e-6
ATTN_HEAD_DIM = 128
ROT_DIM = ATTN_HEAD_DIM // 4
ROPE_THETA = 500000.0
MOBA_BLOCK = 256
MOBA_TOPK = 3
SSD_HEAD_DIM = 64
SSD_STATE = 128
SSD_CONV = 4
SSD_CHUNK = 256
CONV_CARRY_ROWS = 8
LOG2_E = 1.4426950408889634
VT_ROWS = ATTN_HEAD_DIM + 16

V7X_VMEM_LIMIT_BYTES = 56 * 1024 * 1024
LANES = 128

NT_DIMS = (((1,), (1,)), ((), ()))
TN_DIMS = (((0,), (0,)), ((), ()))


def _params(*semantics):
    return pltpu.CompilerParams(dimension_semantics=semantics,
                                vmem_limit_bytes=V7X_VMEM_LIMIT_BYTES)


def _tile(n, pref):
    for t in range(min(n, pref), 0, -LANES):
        if n % t == 0:
            return t
    raise ValueError(f"no lane-aligned tile for {n}")


def _rmsnorm_kernel(x_ref, w_ref, o_ref):
    x = x_ref[...]
    ms = jnp.mean(x * x, axis=-1, keepdims=True)
    o_ref[...] = (x * lax.rsqrt(ms + NORM_EPS) * w_ref[...]).astype(o_ref.dtype)


def _rmsnorm(x, w, out_dtype):
    t, d = x.shape
    tm = _tile(t, 256)
    return pl.pallas_call(
        _rmsnorm_kernel,
        grid=(t // tm,),
        in_specs=[pl.BlockSpec((tm, d), lambda i: (i, 0)),
                  pl.BlockSpec((1, d), lambda i: (0, 0))],
        out_specs=pl.BlockSpec((tm, d), lambda i: (i, 0)),
        out_shape=jax.ShapeDtypeStruct((t, d), out_dtype),
        compiler_params=_params("parallel"),
        name="rmsnorm",
    )(x, w.reshape(1, d))


def _cast_kernel(w_ref, o_ref):
    o_ref[...] = w_ref[...].astype(o_ref.dtype)


def _cast_bf16(w):
    k, n = w.shape
    tr = 64
    assert k % tr == 0
    return pl.pallas_call(
        _cast_kernel,
        grid=(k // tr,),
        in_specs=[pl.BlockSpec((tr, n), lambda i: (i, 0))],
        out_specs=pl.BlockSpec((tr, n), lambda i: (i, 0)),
        out_shape=jax.ShapeDtypeStruct((k, n), BF16),
        compiler_params=_params("parallel"),
        name="cast_w_in",
    )(w)


def _matmul_kernel(x_ref, w_ref, o_ref):
    o_ref[...] = jnp.dot(x_ref[...], w_ref[...], preferred_element_type=F32).astype(o_ref.dtype)


def _matmul(x, w, out_dtype, name, col0=0, n=None):
    m, k = x.shape
    n = w.shape[1] - col0 if n is None else n
    tm, tn = _tile(m, 1024), _tile(math.gcd(n, col0), 1024)
    j0 = col0 // tn
    return pl.pallas_call(
        _matmul_kernel,
        grid=(m // tm, n // tn),
        in_specs=[pl.BlockSpec((tm, k), lambda i, j: (i, 0)),
                  pl.BlockSpec((k, tn), lambda i, j: (0, j0 + j))],
        out_specs=pl.BlockSpec((tm, tn), lambda i, j: (i, j)),
        out_shape=jax.ShapeDtypeStruct((m, n), out_dtype),
        compiler_params=_params("parallel", "arbitrary"),
        name=name,
    )(x, w)


def _rotary_kernel(qk_ref, v_ref, cos_ref, sin_lo_ref, sin_hi_ref, o_ref, kmean_ref, vt_ref, *, n_heads, q_scale):
    cos, sin_lo, sin_hi = cos_ref[...], sin_lo_ref[...], sin_hi_ref[...]
    half = ROT_DIM // 2
    hd = ATTN_HEAD_DIM
    pad_rows = lax.broadcasted_iota(jnp.int32, (VT_ROWS - hd, MOBA_BLOCK), 0)
    ones_row = jnp.where(pad_rows == 0, 1.0, 0.0).astype(vt_ref.dtype)

    def rot(col):
        t = qk_ref[:, pl.ds(col, LANES)].astype(F32)
        return (t * cos + pltpu.roll(t, LANES - half, 1) * sin_lo + pltpu.roll(t, half, 1) * sin_hi)

    def head(h, carry):
        qcol = pl.multiple_of(h * LANES, LANES)
        kcol = pl.multiple_of((n_heads + h) * LANES, LANES)
        o_ref[:, pl.ds(qcol, LANES)] = (rot(qcol) * q_scale).astype(o_ref.dtype)
        k = rot(kcol)
        o_ref[:, pl.ds(kcol, LANES)] = k.astype(o_ref.dtype)
        kmean_ref[0, :, pl.ds(qcol, LANES)] = jnp.mean(k, axis=0, keepdims=True)
        vt_ref[0, h, 0:hd, :] = v_ref[:, pl.ds(qcol, LANES)].astype(F32).T.astype(vt_ref.dtype)
        vt_ref[0, h, hd:VT_ROWS, :] = ones_row
        return carry

    lax.fori_loop(0, n_heads, head, 0)


def _rotary(qkvg, bsz, seq, n_heads):
    t = qkvg.shape[0]
    width = n_heads * ATTN_HEAD_DIM
    nq = seq // MOBA_BLOCK
    half = ROT_DIM // 2
    inv_freq = ROPE_THETA ** (-jnp.arange(half, dtype=F32) * 2.0 / ROT_DIM)
    ang = jnp.arange(seq).astype(F32)[:, None] * inv_freq[None, :]
    cos, sin = jnp.cos(ang), jnp.sin(ang)
    zeros = lambda n: jnp.zeros((seq, n), F32)
    cos_t = jnp.concatenate([cos, cos, jnp.ones((seq, LANES - ROT_DIM), F32)], axis=1)
    sin_lo = jnp.concatenate([-sin, zeros(LANES - half)], axis=1)
    sin_hi = jnp.concatenate([zeros(half), sin, zeros(LANES - ROT_DIM)], axis=1)
    table_spec = pl.BlockSpec((MOBA_BLOCK, LANES), lambda r: (r % nq, 0))
    return pl.pallas_call(
        functools.partial(_rotary_kernel, n_heads=n_heads, q_scale=ATTN_HEAD_DIM ** -0.5 * LOG2_E),
        grid=(t // MOBA_BLOCK,),
        in_specs=[pl.BlockSpec((MOBA_BLOCK, 2 * width), lambda r: (r, 0)),
                  pl.BlockSpec((MOBA_BLOCK, width), lambda r: (r, 2)),
                  table_spec, table_spec, table_spec],
        out_specs=[pl.BlockSpec((MOBA_BLOCK, 2 * width), lambda r: (r, 0)),
                   pl.BlockSpec((1, 1, width), lambda r: (r, 0, 0)),
                   pl.BlockSpec((1, n_heads, VT_ROWS, MOBA_BLOCK), lambda r: (r // nq, 0, 0, r % nq))],
        out_shape=[jax.ShapeDtypeStruct((t, 2 * width), BF16),
                   jax.ShapeDtypeStruct((t // MOBA_BLOCK, 1, width), F32),
                   jax.ShapeDtypeStruct((bsz, n_heads, VT_ROWS, seq), BF16)],
        compiler_params=_params("parallel"),
        name="rotary_kmean",
    )(qkvg, qkvg, cos_t, sin_lo, sin_hi)


def _attn_kernel(q_ref, k_ref, vt_ref, g_ref, kmean_ref, o_ref, *scratch, nq, n_chains):
    sel_scr = scratch[0]
    s_even, s_odd = scratch[1:1 + n_chains], scratch[1 + n_chains:1 + 2 * n_chains]
    acc_scr = scratch[1 + 2 * n_chains:]
    i = pl.program_id(2)
    blk = MOBA_BLOCK
    hd = ATTN_HEAD_DIM
    neg_inf = -jnp.inf
    lead = (i + 1) % 2
    n_pairs = (i + 1 + lead) // 2
    heads = [slice(c * hd, (c + 1) * hd) for c in range(n_chains)]
    qs = [q_ref[:, hs] for hs in heads]

    def issue_scores(pos, dst):
        j = jnp.maximum(pos - lead, 0)
        rows = pl.ds(pl.multiple_of(j * blk, blk), blk)
        col_max = []
        for c in range(n_chains):
            s = lax.dot_general(k_ref[rows, heads[c]], qs[c], NT_DIMS, preferred_element_type=F32)
            dst[c][...] = s
            col_max.append(jnp.max(s, axis=0, keepdims=True))
        return col_max

    def consume(pos, src, col_max, m_run, own):
        j = jnp.maximum(pos - lead, 0)
        cols = pl.ds(pl.multiple_of(j * blk, blk), blk)
        sel_row = jnp.where(pos < lead, nq, j)
        m_out, staged = [], []
        for c in range(n_chains):
            s = src[c][...]
            if own:
                key_pos = lax.broadcasted_iota(jnp.int32, (blk, blk), 0)
                qry_pos = lax.broadcasted_iota(jnp.int32, (blk, blk), 1)
                s = jnp.where(key_pos <= qry_pos, s, neg_inf)
                chosen = None
                m_new = jnp.maximum(m_run[c], jnp.max(s, axis=0, keepdims=True))
                m_sub = m_new
                alpha = jnp.exp2(m_run[c] - m_new)
            else:
                chosen = sel_scr[c, pl.ds(sel_row, 1), :] > 0.0
                m_new = jnp.maximum(m_run[c], jnp.where(chosen, col_max[c], neg_inf))
                seen = m_new > neg_inf
                m_sub = jnp.where(seen, m_new, 0.0)
                alpha = jnp.where(seen, jnp.exp2(m_run[c] - m_new), 0.0)
            p = jnp.exp2(s - m_sub).astype(BF16)
            m_out.append(m_new)
            staged.append((alpha, p, chosen))
        for c in range(n_chains):
            alpha, p, chosen = staged[c]
            pv = jnp.dot(vt_ref[0, c, :, cols], p, preferred_element_type=F32)
            if chosen is not None:
                pv = jnp.where(chosen, pv, 0.0)
            acc_scr[c][...] = alpha * acc_scr[c][...] + pv
        return m_out

    gates = []
    for c in range(n_chains):
        gates.append(lax.dot_general(kmean_ref[0, :, heads[c]], qs[c].astype(F32), NT_DIMS,
                                     preferred_element_type=F32))
    cm0 = issue_scores(0, s_even)
    n_idx = lax.broadcasted_iota(jnp.int32, (nq, blk), 0)
    past = n_idx < i
    for c in range(n_chains):
        work = jnp.where(past, gates[c], neg_inf)
        chosen = jnp.zeros((nq, blk), jnp.bool_)
        for _ in range(MOBA_TOPK):
            top = jnp.max(work, axis=0, keepdims=True)
            first = jnp.min(jnp.where(work == top, n_idx, nq), axis=0, keepdims=True)
            hit = n_idx == first
            chosen = chosen | hit
            work = jnp.where(hit, neg_inf, work)
        sel_scr[c, 0:nq, :] = jnp.where(chosen & past, 1.0, 0.0)
        sel_scr[c, nq:nq + 8, :] = jnp.zeros((8, blk), F32)
        acc_scr[c][...] = jnp.zeros_like(acc_scr[c])

    m0 = [jnp.full((1, blk), neg_inf, F32) for _ in range(n_chains)]

    def pair(t, carry):
        m_run, cm_even = carry
        cm_odd = issue_scores(2 * t + 1, s_odd)
        m_run = consume(2 * t, s_even, cm_even, m_run, own=False)
        cm_even = issue_scores(2 * t + 2, s_even)
        m_run = consume(2 * t + 1, s_odd, cm_odd, m_run, own=False)
        return m_run, cm_even

    m_run, cm_even = lax.fori_loop(0, n_pairs - 1, pair, (m0, cm0))
    last = 2 * (n_pairs - 1)
    issue_scores(last + 1, s_odd)
    m_run = consume(last, s_even, cm_even, m_run, own=False)
    consume(last + 1, s_odd, None, m_run, own=True)

    for c in range(n_chains):
        acc = acc_scr[c][...]
        out = (acc[0:hd, :] / acc[hd:hd + 1, :]).T
        g = g_ref[:, heads[c]].astype(F32)
        o_ref[:, heads[c]] = (out * (g * jax.nn.sigmoid(g))).astype(o_ref.dtype)


ATTN_HEADS_PER_STEP = 8


def _attention(qk_rot, qkvg, kmean, vt, bsz, seq, n_heads):
    t = qk_rot.shape[0]
    nq = seq // MOBA_BLOCK
    width = n_heads * ATTN_HEAD_DIM
    hp = ATTN_HEADS_PER_STEP if n_heads % ATTN_HEADS_PER_STEP == 0 else 1
    n_hg, cw = n_heads // hp, hp * ATTN_HEAD_DIM
    kmean = kmean.reshape(bsz, nq, width)
    score_buf = pltpu.VMEM((MOBA_BLOCK, MOBA_BLOCK), F32)
    return pl.pallas_call(
        functools.partial(_attn_kernel, nq=nq, n_chains=hp),
        grid=(bsz, n_hg, nq),
        in_specs=[pl.BlockSpec((MOBA_BLOCK, cw), lambda b, h, i: (b * nq + i, h)),
                  pl.BlockSpec((seq, cw), lambda b, h, i: (b, n_hg + h)),
                  pl.BlockSpec((1, hp, VT_ROWS, seq), lambda b, h, i: (b, h, 0, 0)),
                  pl.BlockSpec((MOBA_BLOCK, cw), lambda b, h, i: (b * nq + i, 3 * n_hg + h)),
                  pl.BlockSpec((1, nq, cw), lambda b, h, i: (b, 0, h))],
        out_specs=pl.BlockSpec((MOBA_BLOCK, cw), lambda b, h, i: (b * nq + i, h)),
        out_shape=jax.ShapeDtypeStruct((t, width), BF16),
        scratch_shapes=([pltpu.VMEM((hp, nq + 8, MOBA_BLOCK), F32)]
                        + [score_buf] * (2 * hp)
                        + [pltpu.VMEM((VT_ROWS, MOBA_BLOCK), F32)] * hp),
        compiler_params=_params("parallel", "parallel", "arbitrary"),
        name="moba_attention",
    )(qk_rot, qk_rot, vt, qkvg, kmean)


def _xbc_proj_kernel(x_ref, w_ref, cw_ref, cb_ref, o_ref, raw_a, raw_b, carry_scr, *, n_tiles, n_steps, tiles_per_seq):
    t = pl.program_id(0)
    tm, tn = x_ref.shape[0], w_ref.shape[1]
    col_chunk, mm_rows, k_chunk = 256, 256, 512
    n_k = x_ref.shape[1] // k_chunk
    piece = mm_rows // n_k

    def step(src, dst):
        if src is not None:
            row_tile, col_tile = (t - 1) // n_tiles, (t - 1) % n_tiles
            prev = carry_scr[col_tile]
            src[0:CONV_CARRY_ROWS, :] = jnp.where(row_tile % tiles_per_seq == 0, jnp.zeros_like(prev), prev)
            carry_scr[col_tile] = src[tm:tm + CONV_CARRY_ROWS, :]
        for c in range(tn // col_chunk):
            cols = slice(c * col_chunk, (c + 1) * col_chunk)
            for rm in range(tm // mm_rows):
                rows = slice(rm * mm_rows, (rm + 1) * mm_rows)
                part = None
                for kc in range(n_k):
                    if dst is not None:
                        ks = slice(kc * k_chunk, (kc + 1) * k_chunk)
                        d = jnp.dot(x_ref[rows, ks], w_ref[ks, cols], preferred_element_type=F32)
                        part = d if part is None else part + d
                    if src is not None:
                        r0 = rm * mm_rows + kc * piece
                        acc = jnp.broadcast_to(cb_ref[:, cols], (piece, col_chunk))
                        for tap in range(SSD_CONV):
                            off = r0 + CONV_CARRY_ROWS - (SSD_CONV - 1) + tap
                            acc = acc + src[off:off + piece, cols] * cw_ref[tap:tap + 1, cols]
                        o_ref[r0:r0 + piece, cols] = (acc * jax.nn.sigmoid(acc)).astype(o_ref.dtype)
                if dst is not None:
                    dst[CONV_CARRY_ROWS + rm * mm_rows:CONV_CARRY_ROWS + (rm + 1) * mm_rows, cols] = part

    @pl.when(t == 0)
    def _():
        step(None, raw_a)

    @pl.when((t % 2 == 1) & (t < n_steps))
    def _():
        step(raw_a, raw_b)

    @pl.when((t % 2 == 0) & (t > 0) & (t < n_steps))
    def _():
        step(raw_b, raw_a)

    @pl.when(t == n_steps)
    def _():
        step(raw_a if (n_steps - 1) % 2 == 0 else raw_b, None)


def _xbc_proj(u, w, col0, conv_w, conv_b, seq):
    m, k = u.shape
    conv_dim = conv_w.shape[1]
    tm, tn = _tile(math.gcd(m, seq), 1024), _tile(math.gcd(conv_dim, col0), 1024)
    n_tiles, j0 = conv_dim // tn, col0 // tn
    n_steps = (m // tm) * n_tiles
    assert tm % 256 == 0 and tn % 256 == 0 and k % 512 == 0
    proj = lambda t: jnp.minimum(t, n_steps - 1)
    conv = lambda t: jnp.maximum(t - 1, 0)
    raw = pltpu.VMEM((CONV_CARRY_ROWS + tm, tn), F32)
    return pl.pallas_call(
        functools.partial(_xbc_proj_kernel, n_tiles=n_tiles, n_steps=n_steps, tiles_per_seq=seq // tm),
        grid=(n_steps + 1,),
        in_specs=[pl.BlockSpec((tm, k), lambda t: (proj(t) // n_tiles, 0)),
                  pl.BlockSpec((k, tn), lambda t: (0, j0 + proj(t) % n_tiles)),
                  pl.BlockSpec((SSD_CONV, tn), lambda t: (0, conv(t) % n_tiles)),
                  pl.BlockSpec((1, tn), lambda t: (0, conv(t) % n_tiles))],
        out_specs=pl.BlockSpec((tm, tn), lambda t: (conv(t) // n_tiles, conv(t) % n_tiles)),
        out_shape=jax.ShapeDtypeStruct((m, conv_dim), BF16),
        scratch_shapes=[raw, raw, pltpu.VMEM((n_tiles, CONV_CARRY_ROWS, tn), F32)],
        compiler_params=_params("arbitrary"),
        name="in_proj_xbc_conv",
    )(u, w, conv_w, conv_b.reshape(1, conv_dim))


def _split3_bf16(a):
    hi = a.astype(BF16)
    r = a - hi.astype(F32)
    mid = r.astype(BF16)
    lo = (r - mid.astype(F32)).astype(BF16)
    return hi, mid, lo


def _ssd_kernel(z_ref, xs_ref, bc_ref, dt_ref, dtb_ref, alog_ref, dskip_ref, normw_ref,
                o_ref, xe_scr, y_scr, state_scr, *, n_groups, heads_per_group):
    c = pl.program_id(1)
    L = SSD_CHUNK
    gw = heads_per_group * SSD_HEAD_DIM

    @pl.when(c == 0)
    def _():
        state_scr[...] = jnp.zeros_like(state_scr)

    xdt = dt_ref[...] + dtb_ref[...]
    dt = jnp.maximum(xdt, 0.0) + jnp.log1p(jnp.exp(-jnp.abs(xdt)))
    a = dt * (-jnp.exp(alog_ref[...])) * LOG2_E
    t_row = lax.broadcasted_iota(jnp.int32, (L, L), 0)
    t_col = lax.broadcasted_iota(jnp.int32, (L, L), 1)
    causal = t_col <= t_row
    tril = jnp.where(causal, 1.0, 0.0).astype(BF16)
    acum = sum(jnp.dot(tril, part, preferred_element_type=F32) for part in _split3_bf16(a))
    a_tot = acum[L - 1:L, :]
    log_dt = jnp.log2(dt)
    src_t = (acum - log_dt).T
    carry_exp = a_tot - acum + log_dt
    state_decay = jnp.exp2(a_tot)

    lane = lax.broadcasted_iota(jnp.int32, (L, LANES), 1)
    low_half = lane < SSD_HEAD_DIM
    n_bc = n_groups * SSD_STATE

    for g in range(n_groups):
        b_g = bc_ref[:, g * SSD_STATE:(g + 1) * SSD_STATE]
        c_g = bc_ref[:, n_bc + g * SSD_STATE:n_bc + (g + 1) * SSD_STATE]
        cb = lax.dot_general(c_g, b_g, NT_DIMS, preferred_element_type=F32)
        state_g = state_scr[g]
        y_off = jnp.dot(c_g, state_g.astype(BF16), preferred_element_type=F32)
        for pair in range(heads_per_group // 2):
            col0 = g * gw + pair * LANES
            x_pair = xs_ref[:, col0:col0 + LANES].astype(F32)
            x_lo = jnp.where(low_half, x_pair, 0.0).astype(BF16)
            x_hi = jnp.where(low_half, 0.0, x_pair).astype(BF16)
            h0 = g * heads_per_group + 2 * pair
            y_diag = jnp.zeros((L, LANES), F32)
            head_decay = []
            for h, x_half in ((h0, x_lo), (h0 + 1, x_hi)):
                acum_h = jnp.broadcast_to(acum[:, h:h + 1], (L, LANES))
                seg = jnp.concatenate([acum_h] * (L // LANES), axis=1) - src_t[h:h + 1, :]
                w = (cb * jnp.exp2(jnp.where(causal, seg, -jnp.inf))).astype(BF16)
                y_diag = y_diag + jnp.dot(w, x_half, preferred_element_type=F32)
                head_decay.append(jnp.exp2(acum_h))
            in_decay = jnp.where(low_half, head_decay[0], head_decay[1])
            y = (y_diag + in_decay * y_off[:, pair * LANES:(pair + 1) * LANES]
                 + dskip_ref[:, col0:col0 + LANES] * x_pair)
            zg = z_ref[:, col0:col0 + LANES].astype(F32)
            y_scr[:, col0:col0 + LANES] = y * (zg * jax.nn.sigmoid(zg))
            to_end = jnp.where(low_half, jnp.exp2(carry_exp[:, h0:h0 + 1]), jnp.exp2(carry_exp[:, h0 + 1:h0 + 2]))
            xe_scr[g, :, pair * LANES:(pair + 1) * LANES] = (x_pair * to_end).astype(xe_scr.dtype)
            sd = jnp.where(low_half[0:1, :], state_decay[:, h0:h0 + 1], state_decay[:, h0 + 1:h0 + 2])
            state_scr[g, :, pair * LANES:(pair + 1) * LANES] = state_g[:, pair * LANES:(pair + 1) * LANES] * sd
        state_scr[g] = state_scr[g] + lax.dot_general(b_g, xe_scr[g], TN_DIMS, preferred_element_type=F32)
        yg = y_scr[:, g * gw:(g + 1) * gw]
        ms = jnp.mean(yg * yg, axis=-1, keepdims=True)
        o_ref[:, g * gw:(g + 1) * gw] = (yg * lax.rsqrt(ms + NORM_EPS)
                                         * normw_ref[:, g * gw:(g + 1) * gw]).astype(o_ref.dtype)


def _ssd(z, xbc, dt_raw, dt_bias, a_log, d_skip, norm_w, bsz, seq, d_inner, n_heads, n_groups):
    t = z.shape[0]
    nc = seq // SSD_CHUNK
    bc_w = 2 * n_groups * SSD_STATE
    heads_per_group = n_heads // n_groups
    gw = heads_per_group * SSD_HEAD_DIM
    assert heads_per_group % 2 == 0 and d_inner % bc_w == 0 and n_heads <= LANES
    pad_heads = lambda v: jnp.pad(v.astype(F32), (0, LANES - n_heads)).reshape(1, LANES)
    row = lambda b, c: (b * nc + c, 0)
    const = lambda b, c: (0, 0)
    return pl.pallas_call(
        functools.partial(_ssd_kernel, n_groups=n_groups, heads_per_group=heads_per_group),
        grid=(bsz, nc),
        in_specs=[pl.BlockSpec((SSD_CHUNK, d_inner), row),
                  pl.BlockSpec((SSD_CHUNK, d_inner), row),
                  pl.BlockSpec((SSD_CHUNK, bc_w), lambda b, c: (b * nc + c, d_inner // bc_w)),
                  pl.BlockSpec((SSD_CHUNK, LANES), row),
                  pl.BlockSpec((1, LANES), const),
                  pl.BlockSpec((1, LANES), const),
                  pl.BlockSpec((1, d_inner), const),
                  pl.BlockSpec((1, d_inner), const)],
        out_specs=pl.BlockSpec((SSD_CHUNK, d_inner), row),
        out_shape=jax.ShapeDtypeStruct((t, d_inner), BF16),
        scratch_shapes=[pltpu.VMEM((n_groups, SSD_CHUNK, gw), BF16),
                        pltpu.VMEM((SSD_CHUNK, d_inner), F32),
                        pltpu.VMEM((n_groups, SSD_STATE, gw), F32)],
        compiler_params=_params("parallel", "arbitrary"),
        name="ssd_scan",
    )(z, xbc, xbc, dt_raw, pad_heads(dt_bias), pad_heads(a_log),
      jnp.repeat(d_skip.astype(F32), SSD_HEAD_DIM).reshape(1, d_inner), norm_w.reshape(1, d_inner))


def _merge_kernel(oa_ref, wa_ref, ys_ref, ws_ref, ga_ref, gs_ref, ba_ref, bs_ref, o_ref):
    y_attn = jnp.dot(oa_ref[...], wa_ref[...], preferred_element_type=F32)
    y_ssd = jnp.dot(ys_ref[...], ws_ref[...], preferred_element_type=F32)
    gate_a = jax.nn.sigmoid(ga_ref[...].astype(F32) + ba_ref[...])
    gate_s = jax.nn.sigmoid(gs_ref[...].astype(F32) + bs_ref[...])
    o_ref[...] = (gate_a * y_attn + gate_s * y_ssd).astype(o_ref.dtype)


def _merge(o_attn, w_attn_out, y_ssd, w_ssd_out, g_merge, gate_bias):
    t, wa = o_attn.shape
    ws = y_ssd.shape[1]
    d = w_attn_out.shape[1]
    tm, tn = _tile(t, 1024), _tile(d, 512)
    nj = d // tn
    bias = gate_bias.astype(F32).reshape(1, 2 * d)
    return pl.pallas_call(
        _merge_kernel,
        grid=(t // tm, nj),
        in_specs=[pl.BlockSpec((tm, wa), lambda i, j: (i, 0)),
                  pl.BlockSpec((wa, tn), lambda i, j: (0, j)),
                  pl.BlockSpec((tm, ws), lambda i, j: (i, 0)),
                  pl.BlockSpec((ws, tn), lambda i, j: (0, j)),
                  pl.BlockSpec((tm, tn), lambda i, j: (i, j)),
                  pl.BlockSpec((tm, tn), lambda i, j: (i, nj + j)),
                  pl.BlockSpec((1, tn), lambda i, j: (0, j)),
                  pl.BlockSpec((1, tn), lambda i, j: (0, nj + j))],
        out_specs=pl.BlockSpec((tm, tn), lambda i, j: (i, j)),
        out_shape=jax.ShapeDtypeStruct((t, d), BF16),
        compiler_params=_params("parallel", "arbitrary"),
        name="branch_merge",
    )(o_attn, w_attn_out, y_ssd, w_ssd_out, g_merge, g_merge, bias, bias)


def _out_kernel(m_ref, w_ref, x_ref, fw_ref, o_ref, ssq_scr, *, tn, d_model, final_norm):
    j = pl.program_id(1)

    @pl.when(j == 0)
    def _():
        ssq_scr[...] = jnp.zeros_like(ssq_scr)

    h = x_ref[...] + jnp.dot(m_ref[...], w_ref[...], preferred_element_type=F32)
    o_ref[:, pl.ds(pl.multiple_of(j * tn, tn), tn)] = h
    ssq_scr[...] += jnp.sum(h * h, axis=-1, keepdims=True)

    if final_norm:
        @pl.when(j == pl.num_programs(1) - 1)
        def _():
            inv = lax.rsqrt(ssq_scr[...] * (1.0 / d_model) + NORM_EPS)
            o_ref[...] = o_ref[...] * inv * fw_ref[...]


def _out_proj(merged, w_out, x, final_norm_w, final_norm):
    t, d = x.shape
    tm, tn = _tile(t, 512), _tile(d, 512)
    return pl.pallas_call(
        functools.partial(_out_kernel, tn=tn, d_model=d, final_norm=final_norm),
        grid=(t // tm, d // tn),
        in_specs=[pl.BlockSpec((tm, d), lambda i, j: (i, 0)),
                  pl.BlockSpec((d, tn), lambda i, j: (0, j)),
                  pl.BlockSpec((tm, tn), lambda i, j: (i, j)),
                  pl.BlockSpec((1, d), lambda i, j: (0, 0))],
        out_specs=pl.BlockSpec((tm, d), lambda i, j: (i, 0)),
        out_shape=jax.ShapeDtypeStruct((t, d), F32),
        scratch_shapes=[pltpu.VMEM((tm, 1), F32)],
        compiler_params=_params("parallel", "arbitrary"),
        name="out_proj_norm",
    )(merged, w_out, x, final_norm_w.astype(F32).reshape(1, d))


def kernel(x, norm_w, w_in, conv_w, conv_b, dt_bias, a_log, d_skip, ssd_norm_w, w_attn_out, w_ssd_out,
           gate_bias, w_out, final_norm_w):
    bsz, seq, d_model = x.shape
    depth = norm_w.shape[0]
    attn_w = w_attn_out.shape[1]
    d_inner = w_ssd_out.shape[1]
    n_ssd_heads = a_log.shape[1]
    conv_dim = conv_w.shape[2]
    n_groups = (conv_dim - d_inner) // (2 * SSD_STATE)
    n_attn_heads = attn_w // ATTN_HEAD_DIM
    assert seq % MOBA_BLOCK == 0 and seq % SSD_CHUNK == 0
    c_zx, c_dt = 4 * attn_w, 4 * attn_w + d_inner + conv_dim
    c_gm = c_dt + n_ssd_heads

    h = x.reshape(bsz * seq, d_model)
    for l in range(depth):
        w = _cast_bf16(w_in[l])
        u = _rmsnorm(h, norm_w[l], BF16)
        qkvg = _matmul(u, w, BF16, "in_proj_attn", 0, c_zx)
        z = _matmul(u, w, BF16, "in_proj_z", c_zx, d_inner)
        xbc = _xbc_proj(u, w, c_zx + d_inner, conv_w[l], conv_b[l], seq)
        dt_raw = _matmul(u, w, F32, "in_proj_dt", c_dt, LANES)
        g_merge = _matmul(u, w[:, c_gm:], BF16, "in_proj_gate")

        qk_rot, kmean, vt = _rotary(qkvg, bsz, seq, n_attn_heads)
        o_attn = _attention(qk_rot, qkvg, kmean, vt, bsz, seq, n_attn_heads)
        y_ssd = _ssd(z, xbc, dt_raw, dt_bias[l], a_log[l], d_skip[l], ssd_norm_w[l],
                     bsz, seq, d_inner, n_ssd_heads, n_groups)
        merged = _merge(o_attn, w_attn_out[l].astype(BF16), y_ssd, w_ssd_out[l].astype(BF16), g_merge,
                        gate_bias[l])
        h = _out_proj(merged, w_out[l].astype(BF16), h, final_norm_w, final_norm=(l == depth - 1))
    return h.reshape(bsz, seq, d_model)
```

```python
import functools
import math

import jax
import jax.numpy as jnp
from jax import lax
from jax.experimental import pallas as pl
from jax.experimental.pallas import tpu as pltpu

F32 = jnp.float32
BF16 = jnp.bfloat16

NORM_EPS = 1e-6
ATTN_HEAD_DIM = 128
ROT_DIM = ATTN_HEAD_DIM // 4
ROPE_THETA = 500000.0
MOBA_BLOCK = 256
MOBA_TOPK = 3
SSD_HEAD_DIM = 64
SSD_STATE = 128
SSD_CONV = 4
SSD_CHUNK = 256
CONV_CARRY_ROWS = 8
LOG2_E = 1.4426950408889634
VT_ROWS = ATTN_HEAD_DIM + 16

V7X_VMEM_LIMIT_BYTES = 56 * 1024 * 1024
LANES = 128

NT_DIMS = (((1,), (1,)), ((), ()))
TN_DIMS = (((0,), (0,)), ((), ()))


def _params(*semantics):
    return pltpu.CompilerParams(dimension_semantics=semantics,
                                vmem_limit_bytes=V7X_VMEM_LIMIT_BYTES)


def _tile(n, pref):
    for t in range(min(n, pref), 0, -LANES):
        if n % t == 0:
            return t
    raise ValueError(f"no lane-aligned tile for {n}")


def _rmsnorm_kernel(x_ref, w_ref, o_ref):
    x = x_ref[...]
    ms = jnp.mean(x * x, axis=-1, keepdims=True)
    o_ref[...] = (x * lax.rsqrt(ms + NORM_EPS) * w_ref[...]).astype(o_ref.dtype)


def _rmsnorm(x, w, out_dtype):
    t, d = x.shape
    tm = _tile(t, 256)
    return pl.pallas_call(
        _rmsnorm_kernel,
        grid=(t // tm,),
        in_specs=[pl.BlockSpec((tm, d), lambda i: (i, 0)),
                  pl.BlockSpec((1, d), lambda i: (0, 0))],
        out_specs=pl.BlockSpec((tm, d), lambda i: (i, 0)),
        out_shape=jax.ShapeDtypeStruct((t, d), out_dtype),
        compiler_params=_params("parallel"),
        name="rmsnorm",
    )(x, w.reshape(1, d))


def _cast_kernel(w_ref, o_ref):
    o_ref[...] = w_ref[...].astype(o_ref.dtype)


def _cast_rows(wt, row0, n_rows, name):
    k = wt.shape[1]
    tr = _tile(math.gcd(n_rows, row0), 512)
    r0 = row0 // tr
    return pl.pallas_call(
        _cast_kernel,
        grid=(n_rows // tr,),
        in_specs=[pl.BlockSpec((tr, k), lambda i: (r0 + i, 0))],
        out_specs=pl.BlockSpec((tr, k), lambda i: (i, 0)),
        out_shape=jax.ShapeDtypeStruct((n_rows, k), BF16),
        compiler_params=_params("parallel"),
        name=name,
    )(wt)


def _cast_shift_kernel(a_ref, b_ref, o_ref, *, shift):
    tr = o_ref.shape[0]
    o_ref[0:tr - shift, :] = a_ref[shift:tr, :].astype(o_ref.dtype)
    o_ref[tr - shift:tr, :] = b_ref[0:shift, :].astype(o_ref.dtype)


def _cast_rows_unaligned(wt, row0, n_rows, name):
    k = wt.shape[1]
    tr = _tile(n_rows, 512)
    base, shift = row0 // tr, row0 % tr
    assert shift % 16 == 0 and 0 < shift < tr
    return pl.pallas_call(
        functools.partial(_cast_shift_kernel, shift=shift),
        grid=(n_rows // tr,),
        in_specs=[pl.BlockSpec((tr, k), lambda i: (base + i, 0)),
                  pl.BlockSpec((tr, k), lambda i: (base + i + 1, 0))],
        out_specs=pl.BlockSpec((tr, k), lambda i: (i, 0)),
        out_shape=jax.ShapeDtypeStruct((n_rows, k), BF16),
        compiler_params=_params("parallel"),
        name=name,
    )(wt, wt)


def _matmul_kernel(x_ref, wt_ref, o_ref):
    o_ref[...] = lax.dot_general(x_ref[...], wt_ref[...], NT_DIMS, preferred_element_type=F32).astype(o_ref.dtype)


def _matmul(x, wt, out_dtype, name, row0=0, n=None):
    m, k = x.shape
    n = wt.shape[0] - row0 if n is None else n
    tm, tn = _tile(m, 1024), _tile(math.gcd(n, row0), 1024)
    j0 = row0 // tn
    return pl.pallas_call(
        _matmul_kernel,
        grid=(m // tm, n // tn),
        in_specs=[pl.BlockSpec((tm, k), lambda i, j: (i, 0)),
                  pl.BlockSpec((tn, k), lambda i, j: (j0 + j, 0))],
        out_specs=pl.BlockSpec((tm, tn), lambda i, j: (i, j)),
        out_shape=jax.ShapeDtypeStruct((m, n), out_dtype),
        compiler_params=_params("parallel", "arbitrary"),
        name=name,
    )(x, wt)


def _rotary_kernel(qk_ref, v_ref, cos_ref, sin_lo_ref, sin_hi_ref, o_ref, kmean_ref, vt_ref, *, n_heads, q_scale):
    cos, sin_lo, sin_hi = cos_ref[...], sin_lo_ref[...], sin_hi_ref[...]
    half = ROT_DIM // 2
    hd = ATTN_HEAD_DIM
    pad_rows = lax.broadcasted_iota(jnp.int32, (VT_ROWS - hd, MOBA_BLOCK), 0)
    ones_row = jnp.where(pad_rows == 0, 1.0, 0.0).astype(vt_ref.dtype)

    def rot(col):
        t = qk_ref[:, pl.ds(col, LANES)].astype(F32)
        return (t * cos + pltpu.roll(t, LANES - half, 1) * sin_lo + pltpu.roll(t, half, 1) * sin_hi)

    def head(h, carry):
        qcol = pl.multiple_of(h * LANES, LANES)
        kcol = pl.multiple_of((n_heads + h) * LANES, LANES)
        o_ref[:, pl.ds(qcol, LANES)] = (rot(qcol) * q_scale).astype(o_ref.dtype)
        k = rot(kcol)
        o_ref[:, pl.ds(kcol, LANES)] = k.astype(o_ref.dtype)
        kmean_ref[0, :, pl.ds(qcol, LANES)] = jnp.mean(k, axis=0, keepdims=True)
        vt_ref[0, h, 0:hd, :] = v_ref[:, pl.ds(qcol, LANES)].astype(F32).T.astype(vt_ref.dtype)
        vt_ref[0, h, hd:VT_ROWS, :] = ones_row
        return carry

    lax.fori_loop(0, n_heads, head, 0)


def _rotary(qkvg, bsz, seq, n_heads):
    t = qkvg.shape[0]
    width = n_heads * ATTN_HEAD_DIM
    nq = seq // MOBA_BLOCK
    half = ROT_DIM // 2
    inv_freq = ROPE_THETA ** (-jnp.arange(half, dtype=F32) * 2.0 / ROT_DIM)
    ang = jnp.arange(seq).astype(F32)[:, None] * inv_freq[None, :]
    cos, sin = jnp.cos(ang), jnp.sin(ang)
    zeros = lambda n: jnp.zeros((seq, n), F32)
    cos_t = jnp.concatenate([cos, cos, jnp.ones((seq, LANES - ROT_DIM), F32)], axis=1)
    sin_lo = jnp.concatenate([-sin, zeros(LANES - half)], axis=1)
    sin_hi = jnp.concatenate([zeros(half), sin, zeros(LANES - ROT_DIM)], axis=1)
    table_spec = pl.BlockSpec((MOBA_BLOCK, LANES), lambda r: (r % nq, 0))
    return pl.pallas_call(
        functools.partial(_rotary_kernel, n_heads=n_heads, q_scale=ATTN_HEAD_DIM ** -0.5 * LOG2_E),
        grid=(t // MOBA_BLOCK,),
        in_specs=[pl.BlockSpec((MOBA_BLOCK, 2 * width), lambda r: (r, 0)),
                  pl.BlockSpec((MOBA_BLOCK, width), lambda r: (r, 2)),
                  table_spec, table_spec, table_spec],
        out_specs=[pl.BlockSpec((MOBA_BLOCK, 2 * width), lambda r: (r, 0)),
                   pl.BlockSpec((1, 1, width), lambda r: (r, 0, 0)),
                   pl.BlockSpec((1, n_heads, VT_ROWS, MOBA_BLOCK), lambda r: (r // nq, 0, 0, r % nq))],
        out_shape=[jax.ShapeDtypeStruct((t, 2 * width), BF16),
                   jax.ShapeDtypeStruct((t // MOBA_BLOCK, 1, width), F32),
                   jax.ShapeDtypeStruct((bsz, n_heads, VT_ROWS, seq), BF16)],
        compiler_params=_params("parallel"),
        name="rotary_kmean",
    )(qkvg, qkvg, cos_t, sin_lo, sin_hi)


def _attn_kernel(q_ref, k_ref, vt_ref, g_ref, kmean_ref, o_ref, *scratch, nq, n_chains):
    sel_scr = scratch[0]
    s_even, s_odd = scratch[1:1 + n_chains], scratch[1 + n_chains:1 + 2 * n_chains]
    acc_scr = scratch[1 + 2 * n_chains:]
    i = pl.program_id(2)
    blk = MOBA_BLOCK
    hd = ATTN_HEAD_DIM
    neg_inf = -jnp.inf
    lead = (i + 1) % 2
    n_pairs = (i + 1 + lead) // 2
    heads = [slice(c * hd, (c + 1) * hd) for c in range(n_chains)]
    qs = [q_ref[:, hs] for hs in heads]

    def issue_scores(pos, dst):
        j = jnp.maximum(pos - lead, 0)
        rows = pl.ds(pl.multiple_of(j * blk, blk), blk)
        col_max = []
        for c in range(n_chains):
            s = lax.dot_general(k_ref[rows, heads[c]], qs[c], NT_DIMS, preferred_element_type=F32)
            dst[c][...] = s
            col_max.append(jnp.max(s, axis=0, keepdims=True))
        return col_max

    def consume(pos, src, col_max, m_run, own):
        j = jnp.maximum(pos - lead, 0)
        cols = pl.ds(pl.multiple_of(j * blk, blk), blk)
        sel_row = jnp.where(pos < lead, nq, j)
        m_out, staged = [], []
        for c in range(n_chains):
            s = src[c][...]
            if own:
                key_pos = lax.broadcasted_iota(jnp.int32, (blk, blk), 0)
                qry_pos = lax.broadcasted_iota(jnp.int32, (blk, blk), 1)
                s = jnp.where(key_pos <= qry_pos, s, neg_inf)
                chosen = None
                m_new = jnp.maximum(m_run[c], jnp.max(s, axis=0, keepdims=True))
                m_sub = m_new
                alpha = jnp.exp2(m_run[c] - m_new)
            else:
                chosen = sel_scr[c, pl.ds(sel_row, 1), :] > 0.0
                m_new = jnp.maximum(m_run[c], jnp.where(chosen, col_max[c], neg_inf))
                seen = m_new > neg_inf
                m_sub = jnp.where(seen, m_new, 0.0)
                alpha = jnp.where(seen, jnp.exp2(m_run[c] - m_new), 0.0)
            p = jnp.exp2(s - m_sub).astype(BF16)
            m_out.append(m_new)
            staged.append((alpha, p, chosen))
        for c in range(n_chains):
            alpha, p, chosen = staged[c]
            pv = jnp.dot(vt_ref[0, c, :, cols], p, preferred_element_type=F32)
            if chosen is not None:
                pv = jnp.where(chosen, pv, 0.0)
            acc_scr[c][...] = alpha * acc_scr[c][...] + pv
        return m_out

    gates = []
    for c in range(n_chains):
        gates.append(lax.dot_general(kmean_ref[0, :, heads[c]], qs[c].astype(F32), NT_DIMS,
                                     preferred_element_type=F32))
    cm0 = issue_scores(0, s_even)
    n_idx = lax.broadcasted_iota(jnp.int32, (nq, blk), 0)
    past = n_idx < i
    for c in range(n_chains):
        work = jnp.where(past, gates[c], neg_inf)
        chosen = jnp.zeros((nq, blk), jnp.bool_)
        for _ in range(MOBA_TOPK):
            top = jnp.max(work, axis=0, keepdims=True)
            first = jnp.min(jnp.where(work == top, n_idx, nq), axis=0, keepdims=True)
            hit = n_idx == first
            chosen = chosen | hit
            work = jnp.where(hit, neg_inf, work)
        sel_scr[c, 0:nq, :] = jnp.where(chosen & past, 1.0, 0.0)
        sel_scr[c, nq:nq + 8, :] = jnp.zeros((8, blk), F32)
        acc_scr[c][...] = jnp.zeros_like(acc_scr[c])

    m0 = [jnp.full((1, blk), neg_inf, F32) for _ in range(n_chains)]

    def pair(t, carry):
        m_run, cm_even = carry
        cm_odd = issue_scores(2 * t + 1, s_odd)
        m_run = consume(2 * t, s_even, cm_even, m_run, own=False)
        cm_even = issue_scores(2 * t + 2, s_even)
        m_run = consume(2 * t + 1, s_odd, cm_odd, m_run, own=False)
        return m_run, cm_even

    m_run, cm_even = lax.fori_loop(0, n_pairs - 1, pair, (m0, cm0))
    last = 2 * (n_pairs - 1)
    issue_scores(last + 1, s_odd)
    m_run = consume(last, s_even, cm_even, m_run, own=False)
    consume(last + 1, s_odd, None, m_run, own=True)

    for c in range(n_chains):
        acc = acc_scr[c][...]
        out = (acc[0:hd, :] / acc[hd:hd + 1, :]).T
        g = g_ref[:, heads[c]].astype(F32)
        o_ref[:, heads[c]] = (out * (g * jax.nn.sigmoid(g))).astype(o_ref.dtype)


ATTN_HEADS_PER_STEP = 8


def _attention(qk_rot, qkvg, kmean, vt, bsz, seq, n_heads):
    t = qk_rot.shape[0]
    nq = seq // MOBA_BLOCK
    width = n_heads * ATTN_HEAD_DIM
    hp = ATTN_HEADS_PER_STEP if n_heads % ATTN_HEADS_PER_STEP == 0 else 1
    n_hg, cw = n_heads // hp, hp * ATTN_HEAD_DIM
    kmean = kmean.reshape(bsz, nq, width)
    score_buf = pltpu.VMEM((MOBA_BLOCK, MOBA_BLOCK), F32)
    return pl.pallas_call(
        functools.partial(_attn_kernel, nq=nq, n_chains=hp),
        grid=(bsz, n_hg, nq),
        in_specs=[pl.BlockSpec((MOBA_BLOCK, cw), lambda b, h, i: (b * nq + i, h)),
                  pl.BlockSpec((seq, cw), lambda b, h, i: (b, n_hg + h)),
                  pl.BlockSpec((1, hp, VT_ROWS, seq), lambda b, h, i: (b, h, 0, 0)),
                  pl.BlockSpec((MOBA_BLOCK, cw), lambda b, h, i: (b * nq + i, 3 * n_hg + h)),
                  pl.BlockSpec((1, nq, cw), lambda b, h, i: (b, 0, h))],
        out_specs=pl.BlockSpec((MOBA_BLOCK, cw), lambda b, h, i: (b * nq + i, h)),
        out_shape=jax.ShapeDtypeStruct((t, width), BF16),
        scratch_shapes=([pltpu.VMEM((hp, nq + 8, MOBA_BLOCK), F32)]
                        + [score_buf] * (2 * hp)
                        + [pltpu.VMEM((VT_ROWS, MOBA_BLOCK), F32)] * hp),
        compiler_params=_params("parallel", "parallel", "arbitrary"),
        name="moba_attention",
    )(qk_rot, qk_rot, vt, qkvg, kmean)


def _xbc_proj_kernel(x_ref, w_ref, cw_ref, cb_ref, o_ref, raw_a, raw_b, carry_scr, *, n_tiles, n_steps, tiles_per_seq):
    t = pl.program_id(0)
    tm, tn = x_ref.shape[0], w_ref.shape[0]
    col_chunk, mm_rows, k_chunk = 256, 256, 512
    n_k = x_ref.shape[1] // k_chunk
    piece = mm_rows // n_k

    def step(src, dst):
        if src is not None:
            row_tile, col_tile = (t - 1) // n_tiles, (t - 1) % n_tiles
            prev = carry_scr[col_tile]
            src[0:CONV_CARRY_ROWS, :] = jnp.where(row_tile % tiles_per_seq == 0, jnp.zeros_like(prev), prev)
            carry_scr[col_tile] = src[tm:tm + CONV_CARRY_ROWS, :]
        for c in range(tn // col_chunk):
            cols = slice(c * col_chunk, (c + 1) * col_chunk)
            for rm in range(tm // mm_rows):
                rows = slice(rm * mm_rows, (rm + 1) * mm_rows)
                part = None
                for kc in range(n_k):
                    if dst is not None:
                        ks = slice(kc * k_chunk, (kc + 1) * k_chunk)
                        d = lax.dot_general(x_ref[rows, ks], w_ref[cols, ks], NT_DIMS, preferred_element_type=F32)
                        part = d if part is None else part + d
                    if src is not None:
                        r0 = rm * mm_rows + kc * piece
                        acc = jnp.broadcast_to(cb_ref[:, cols], (piece, col_chunk))
                        for tap in range(SSD_CONV):
                            off = r0 + CONV_CARRY_ROWS - (SSD_CONV - 1) + tap
                            acc = acc + src[off:off + piece, cols] * cw_ref[tap:tap + 1, cols]
                        o_ref[r0:r0 + piece, cols] = (acc * jax.nn.sigmoid(acc)).astype(o_ref.dtype)
                if dst is not None:
                    dst[CONV_CARRY_ROWS + rm * mm_rows:CONV_CARRY_ROWS + (rm + 1) * mm_rows, cols] = part

    @pl.when(t == 0)
    def _():
        step(None, raw_a)

    @pl.when((t % 2 == 1) & (t < n_steps))
    def _():
        step(raw_a, raw_b)

    @pl.when((t % 2 == 0) & (t > 0) & (t < n_steps))
    def _():
        step(raw_b, raw_a)

    @pl.when(t == n_steps)
    def _():
        step(raw_a if (n_steps - 1) % 2 == 0 else raw_b, None)


def _xbc_proj(u, wt, col0, conv_w, conv_b, seq):
    m, k = u.shape
    conv_dim = conv_w.shape[1]
    tm, tn = _tile(math.gcd(m, seq), 1024), _tile(math.gcd(conv_dim, col0), 1024)
    n_tiles, j0 = conv_dim // tn, col0 // tn
    n_steps = (m // tm) * n_tiles
    assert tm % 256 == 0 and tn % 256 == 0 and k % 512 == 0
    proj = lambda t: jnp.minimum(t, n_steps - 1)
    conv = lambda t: jnp.maximum(t - 1, 0)
    raw = pltpu.VMEM((CONV_CARRY_ROWS + tm, tn), F32)
    return pl.pallas_call(
        functools.partial(_xbc_proj_kernel, n_tiles=n_tiles, n_steps=n_steps, tiles_per_seq=seq // tm),
        grid=(n_steps + 1,),
        in_specs=[pl.BlockSpec((tm, k), lambda t: (proj(t) // n_tiles, 0)),
                  pl.BlockSpec((tn, k), lambda t: (j0 + proj(t) % n_tiles, 0)),
                  pl.BlockSpec((SSD_CONV, tn), lambda t: (0, conv(t) % n_tiles)),
                  pl.BlockSpec((1, tn), lambda t: (0, conv(t) % n_tiles))],
        out_specs=pl.BlockSpec((tm, tn), lambda t: (conv(t) // n_tiles, conv(t) % n_tiles)),
        out_shape=jax.ShapeDtypeStruct((m, conv_dim), BF16),
        scratch_shapes=[raw, raw, pltpu.VMEM((n_tiles, CONV_CARRY_ROWS, tn), F32)],
        compiler_params=_params("arbitrary"),
        name="in_proj_xbc_conv",
    )(u, wt, conv_w, conv_b.reshape(1, conv_dim))


def _split3_bf16(a):
    hi = a.astype(BF16)
    r = a - hi.astype(F32)
    mid = r.astype(BF16)
    lo = (r - mid.astype(F32)).astype(BF16)
    return hi, mid, lo


def _ssd_kernel(z_ref, xs_ref, bc_ref, dt_ref, dtb_ref, alog_ref, dskip_ref, normw_ref,
                o_ref, xe_scr, y_scr, state_scr, *, n_groups, heads_per_group):
    c = pl.program_id(1)
    L = SSD_CHUNK
    gw = heads_per_group * SSD_HEAD_DIM

    @pl.when(c == 0)
    def _():
        state_scr[...] = jnp.zeros_like(state_scr)

    xdt = dt_ref[...] + dtb_ref[...]
    dt = jnp.maximum(xdt, 0.0) + jnp.log1p(jnp.exp(-jnp.abs(xdt)))
    a = dt * (-jnp.exp(alog_ref[...])) * LOG2_E
    t_row = lax.broadcasted_iota(jnp.int32, (L, L), 0)
    t_col = lax.broadcasted_iota(jnp.int32, (L, L), 1)
    causal = t_col <= t_row
    tril = jnp.where(causal, 1.0, 0.0).astype(BF16)
    acum = sum(jnp.dot(tril, part, preferred_element_type=F32) for part in _split3_bf16(a))
    a_tot = acum[L - 1:L, :]
    log_dt = jnp.log2(dt)
    src_t = (acum - log_dt).T
    carry_exp = a_tot - acum + log_dt
    state_decay = jnp.exp2(a_tot)

    lane = lax.broadcasted_iota(jnp.int32, (L, LANES), 1)
    low_half = lane < SSD_HEAD_DIM
    n_bc = n_groups * SSD_STATE

    for g in range(n_groups):
        b_g = bc_ref[:, g * SSD_STATE:(g + 1) * SSD_STATE]
        c_g = bc_ref[:, n_bc + g * SSD_STATE:n_bc + (g + 1) * SSD_STATE]
        cb = lax.dot_general(c_g, b_g, NT_DIMS, preferred_element_type=F32)
        state_g = state_scr[g]
        y_off = jnp.dot(c_g, state_g.astype(BF16), preferred_element_type=F32)
        for pair in range(heads_per_group // 2):
            col0 = g * gw + pair * LANES
            x_pair = xs_ref[:, col0:col0 + LANES].astype(F32)
            x_lo = jnp.where(low_half, x_pair, 0.0).astype(BF16)
            x_hi = jnp.where(low_half, 0.0, x_pair).astype(BF16)
            h0 = g * heads_per_group + 2 * pair
            y_diag = jnp.zeros((L, LANES), F32)
            head_decay = []
            for h, x_half in ((h0, x_lo), (h0 + 1, x_hi)):
                acum_h = jnp.broadcast_to(acum[:, h:h + 1], (L, LANES))
                seg = jnp.concatenate([acum_h] * (L // LANES), axis=1) - src_t[h:h + 1, :]
                w = (cb * jnp.exp2(jnp.where(causal, seg, -jnp.inf))).astype(BF16)
                y_diag = y_diag + jnp.dot(w, x_half, preferred_element_type=F32)
                head_decay.append(jnp.exp2(acum_h))
            in_decay = jnp.where(low_half, head_decay[0], head_decay[1])
            y = (y_diag + in_decay * y_off[:, pair * LANES:(pair + 1) * LANES]
                 + dskip_ref[:, col0:col0 + LANES] * x_pair)
            zg = z_ref[:, col0:col0 + LANES].astype(F32)
            y_scr[:, col0:col0 + LANES] = y * (zg * jax.nn.sigmoid(zg))
            to_end = jnp.where(low_half, jnp.exp2(carry_exp[:, h0:h0 + 1]), jnp.exp2(carry_exp[:, h0 + 1:h0 + 2]))
            xe_scr[g, :, pair * LANES:(pair + 1) * LANES] = (x_pair * to_end).astype(xe_scr.dtype)
            sd = jnp.where(low_half[0:1, :], state_decay[:, h0:h0 + 1], state_decay[:, h0 + 1:h0 + 2])
            state_scr[g, :, pair * LANES:(pair + 1) * LANES] = state_g[:, pair * LANES:(pair + 1) * LANES] * sd
        state_scr[g] = state_scr[g] + lax.dot_general(b_g, xe_scr[g], TN_DIMS, preferred_element_type=F32)
        yg = y_scr[:, g * gw:(g + 1) * gw]
        ms = jnp.mean(yg * yg, axis=-1, keepdims=True)
        o_ref[:, g * gw:(g + 1) * gw] = (yg * lax.rsqrt(ms + NORM_EPS)
                                         * normw_ref[:, g * gw:(g + 1) * gw]).astype(o_ref.dtype)


def _ssd(z, xbc, dt_raw, dt_bias, a_log, d_skip, norm_w, bsz, seq, d_inner, n_heads, n_groups):
    t = z.shape[0]
    nc = seq // SSD_CHUNK
    bc_w = 2 * n_groups * SSD_STATE
    heads_per_group = n_heads // n_groups
    gw = heads_per_group * SSD_HEAD_DIM
    assert heads_per_group % 2 == 0 and d_inner % bc_w == 0 and n_heads <= LANES
    pad_heads = lambda v: jnp.pad(v.astype(F32), (0, LANES - n_heads)).reshape(1, LANES)
    row = lambda b, c: (b * nc + c, 0)
    const = lambda b, c: (0, 0)
    return pl.pallas_call(
        functools.partial(_ssd_kernel, n_groups=n_groups, heads_per_group=heads_per_group),
        grid=(bsz, nc),
        in_specs=[pl.BlockSpec((SSD_CHUNK, d_inner), row),
                  pl.BlockSpec((SSD_CHUNK, d_inner), row),
                  pl.BlockSpec((SSD_CHUNK, bc_w), lambda b, c: (b * nc + c, d_inner // bc_w)),
                  pl.BlockSpec((SSD_CHUNK, LANES), row),
                  pl.BlockSpec((1, LANES), const),
                  pl.BlockSpec((1, LANES), const),
                  pl.BlockSpec((1, d_inner), const),
                  pl.BlockSpec((1, d_inner), const)],
        out_specs=pl.BlockSpec((SSD_CHUNK, d_inner), row),
        out_shape=jax.ShapeDtypeStruct((t, d_inner), BF16),
        scratch_shapes=[pltpu.VMEM((n_groups, SSD_CHUNK, gw), BF16),
                        pltpu.VMEM((SSD_CHUNK, d_inner), F32),
                        pltpu.VMEM((n_groups, SSD_STATE, gw), F32)],
        compiler_params=_params("parallel", "arbitrary"),
        name="ssd_scan",
    )(z, xbc, xbc, dt_raw, pad_heads(dt_bias), pad_heads(a_log),
      jnp.repeat(d_skip.astype(F32), SSD_HEAD_DIM).reshape(1, d_inner), norm_w.reshape(1, d_inner))


def _merge_kernel(oa_ref, wa_ref, ys_ref, ws_ref, ga_ref, gs_ref, ba_ref, bs_ref, o_ref):
    y_attn = jnp.dot(oa_ref[...], wa_ref[...], preferred_element_type=F32)
    y_ssd = jnp.dot(ys_ref[...], ws_ref[...], preferred_element_type=F32)
    gate_a = jax.nn.sigmoid(ga_ref[...].astype(F32) + ba_ref[...])
    gate_s = jax.nn.sigmoid(gs_ref[...].astype(F32) + bs_ref[...])
    o_ref[...] = (gate_a * y_attn + gate_s * y_ssd).astype(o_ref.dtype)


def _merge(o_attn, w_attn_out, y_ssd, w_ssd_out, g_merge, gate_bias):
    t, wa = o_attn.shape
    ws = y_ssd.shape[1]
    d = w_attn_out.shape[1]
    tm, tn = _tile(t, 1024), _tile(d, 512)
    nj = d // tn
    bias = gate_bias.astype(F32).reshape(1, 2 * d)
    return pl.pallas_call(
        _merge_kernel,
        grid=(t // tm, nj),
        in_specs=[pl.BlockSpec((tm, wa), lambda i, j: (i, 0)),
                  pl.BlockSpec((wa, tn), lambda i, j: (0, j)),
                  pl.BlockSpec((tm, ws), lambda i, j: (i, 0)),
                  pl.BlockSpec((ws, tn), lambda i, j: (0, j)),
                  pl.BlockSpec((tm, tn), lambda i, j: (i, j)),
                  pl.BlockSpec((tm, tn), lambda i, j: (i, nj + j)),
                  pl.BlockSpec((1, tn), lambda i, j: (0, j)),
                  pl.BlockSpec((1, tn), lambda i, j: (0, nj + j))],
        out_specs=pl.BlockSpec((tm, tn), lambda i, j: (i, j)),
        out_shape=jax.ShapeDtypeStruct((t, d), BF16),
        compiler_params=_params("parallel", "arbitrary"),
        name="branch_merge",
    )(o_attn, w_attn_out, y_ssd, w_ssd_out, g_merge, g_merge, bias, bias)


def _out_kernel(m_ref, w_ref, x_ref, fw_ref, o_ref, ssq_scr, *, tn, d_model, final_norm):
    j = pl.program_id(1)

    @pl.when(j == 0)
    def _():
        ssq_scr[...] = jnp.zeros_like(ssq_scr)

    h = x_ref[...] + jnp.dot(m_ref[...], w_ref[...], preferred_element_type=F32)
    o_ref[:, pl.ds(pl.multiple_of(j * tn, tn), tn)] = h
    ssq_scr[...] += jnp.sum(h * h, axis=-1, keepdims=True)

    if final_norm:
        @pl.when(j == pl.num_programs(1) - 1)
        def _():
            inv = lax.rsqrt(ssq_scr[...] * (1.0 / d_model) + NORM_EPS)
            o_ref[...] = o_ref[...] * inv * fw_ref[...]


def _out_proj(merged, w_out, x, final_norm_w, final_norm):
    t, d = x.shape
    tm, tn = _tile(t, 512), _tile(d, 512)
    return pl.pallas_call(
        functools.partial(_out_kernel, tn=tn, d_model=d, final_norm=final_norm),
        grid=(t // tm, d // tn),
        in_specs=[pl.BlockSpec((tm, d), lambda i, j: (i, 0)),
                  pl.BlockSpec((d, tn), lambda i, j: (0, j)),
                  pl.BlockSpec((tm, tn), lambda i, j: (i, j)),
                  pl.BlockSpec((1, d), lambda i, j: (0, 0))],
        out_specs=pl.BlockSpec((tm, d), lambda i, j: (i, 0)),
        out_shape=jax.ShapeDtypeStruct((t, d), F32),
        scratch_shapes=[pltpu.VMEM((tm, 1), F32)],
        compiler_params=_params("parallel", "arbitrary"),
        name="out_proj_norm",
    )(merged, w_out, x, final_norm_w.astype(F32).reshape(1, d))


def kernel(x, norm_w, w_in, conv_w, conv_b, dt_bias, a_log, d_skip, ssd_norm_w, w_attn_out, w_ssd_out,
           gate_bias, w_out, final_norm_w):
    bsz, seq, d_model = x.shape
    depth = norm_w.shape[0]
    attn_w = w_attn_out.shape[1]
    d_inner = w_ssd_out.shape[1]
    n_ssd_heads = a_log.shape[1]
    conv_dim = conv_w.shape[2]
    n_groups = (conv_dim - d_inner) // (2 * SSD_STATE)
    n_attn_heads = attn_w // ATTN_HEAD_DIM
    assert seq % MOBA_BLOCK == 0 and seq % SSD_CHUNK == 0
    c_zx, c_dt = 4 * attn_w, 4 * attn_w + d_inner + conv_dim
    c_gm = c_dt + n_ssd_heads

    h = x.reshape(bsz * seq, d_model)
    for l in range(depth):
        wt = jnp.swapaxes(w_in[l], 0, 1)
        w_main = _cast_rows(wt, 0, c_dt, "cast_w_main")
        w_dt = _cast_rows(wt, c_dt, LANES, "cast_w_dt")
        w_gate = _cast_rows_unaligned(wt, c_gm, wt.shape[0] - c_gm, "cast_w_gate")
        u = _rmsnorm(h, norm_w[l], BF16)
        qkvg = _matmul(u, w_main, BF16, "in_proj_attn", 0, c_zx)
        z = _matmul(u, w_main, BF16, "in_proj_z", c_zx, d_inner)
        xbc = _xbc_proj(u, w_main, c_zx + d_inner, conv_w[l], conv_b[l], seq)
        dt_raw = _matmul(u, w_dt, F32, "in_proj_dt")
        g_merge = _matmul(u, w_gate, BF16, "in_proj_gate")

        qk_rot, kmean, vt = _rotary(qkvg, bsz, seq, n_attn_heads)
        o_attn = _attention(qk_rot, qkvg, kmean, vt, bsz, seq, n_attn_heads)
        y_ssd = _ssd(z, xbc, dt_raw, dt_bias[l], a_log[l], d_skip[l], ssd_norm_w[l],
                     bsz, seq, d_inner, n_ssd_heads, n_groups)
        merged = _merge(o_attn, w_attn_out[l].astype(BF16), y_ssd, w_ssd_out[l].astype(BF16), g_merge,
                        gate_bias[l])
        h = _out_proj(merged, w_out[l].astype(BF16), h, final_norm_w, final_norm=(l == depth - 1))
    return h.reshape(bsz, seq, d_model)
```

```python
import functools
import math

import jax
import jax.numpy as jnp
from jax import lax
from jax.experimental import pallas as pl
from jax.experimental.pallas import tpu as pltpu

F32 = jnp.float32
BF16 = jnp.bfloat16

NORM_EPS = 1e-6
ATTN_HEAD_DIM = 128
ROT_DIM = ATTN_HEAD_DIM // 4
ROPE_THETA = 500000.0
MOBA_BLOCK = 256
MOBA_TOPK = 3
SSD_HEAD_DIM = 64
SSD_STATE = 128
SSD_CONV = 4
SSD_CHUNK = 256
CONV_CARRY_ROWS = 8
LOG2_E = 1.4426950408889634
VT_ROWS = ATTN_HEAD_DIM + 16

V7X_VMEM_LIMIT_BYTES = 56 * 1024 * 1024
LANES = 128

NT_DIMS = (((1,), (1,)), ((), ()))
TN_DIMS = (((0,), (0,)), ((), ()))


def _params(*semantics):
    return pltpu.CompilerParams(dimension_semantics=semantics,
                                vmem_limit_bytes=V7X_VMEM_LIMIT_BYTES)


def _tile(n, pref):
    for t in range(min(n, pref), 0, -LANES):
        if n % t == 0:
            return t
    raise ValueError(f"no lane-aligned tile for {n}")


def _rmsnorm_kernel(x_ref, w_ref, o_ref):
    x = x_ref[...]
    ms = jnp.mean(x * x, axis=-1, keepdims=True)
    o_ref[...] = (x * lax.rsqrt(ms + NORM_EPS) * w_ref[...]).astype(o_ref.dtype)


def _rmsnorm(x, w, out_dtype):
    t, d = x.shape
    tm = _tile(t, 256)
    return pl.pallas_call(
        _rmsnorm_kernel,
        grid=(t // tm,),
        in_specs=[pl.BlockSpec((tm, d), lambda i: (i, 0)),
                  pl.BlockSpec((1, d), lambda i: (0, 0))],
        out_specs=pl.BlockSpec((tm, d), lambda i: (i, 0)),
        out_shape=jax.ShapeDtypeStruct((t, d), out_dtype),
        compiler_params=_params("parallel"),
        name="rmsnorm",
    )(x, w.reshape(1, d))


def _cast_kernel(w_ref, o_ref):
    o_ref[...] = w_ref[...].astype(o_ref.dtype)


def _cast_rows(wt, row0, n_rows, name):
    k = wt.shape[1]
    tr = _tile(math.gcd(n_rows, row0), 512)
    r0 = row0 // tr
    return pl.pallas_call(
        _cast_kernel,
        grid=(n_rows // tr,),
        in_specs=[pl.BlockSpec((tr, k), lambda i: (r0 + i, 0))],
        out_specs=pl.BlockSpec((tr, k), lambda i: (i, 0)),
        out_shape=jax.ShapeDtypeStruct((n_rows, k), BF16),
        compiler_params=_params("parallel"),
        name=name,
    )(wt)


def _cast_shift_kernel(a_ref, b_ref, o_ref, *, shift):
    tr = o_ref.shape[0]
    o_ref[0:tr - shift, :] = a_ref[shift:tr, :].astype(o_ref.dtype)
    o_ref[tr - shift:tr, :] = b_ref[0:shift, :].astype(o_ref.dtype)


def _cast_rows_unaligned(wt, row0, n_rows, name):
    k = wt.shape[1]
    tr = _tile(n_rows, 512)
    base, shift = row0 // tr, row0 % tr
    assert shift % 16 == 0 and 0 < shift < tr
    return pl.pallas_call(
        functools.partial(_cast_shift_kernel, shift=shift),
        grid=(n_rows // tr,),
        in_specs=[pl.BlockSpec((tr, k), lambda i: (base + i, 0)),
                  pl.BlockSpec((tr, k), lambda i: (base + i + 1, 0))],
        out_specs=pl.BlockSpec((tr, k), lambda i: (i, 0)),
        out_shape=jax.ShapeDtypeStruct((n_rows, k), BF16),
        compiler_params=_params("parallel"),
        name=name,
    )(wt, wt)


def _matmul_kernel(x_ref, wt_ref, o_ref):
    o_ref[...] = lax.dot_general(x_ref[...], wt_ref[...], NT_DIMS, preferred_element_type=F32).astype(o_ref.dtype)


def _matmul(x, wt, out_dtype, name, row0=0, n=None):
    m, k = x.shape
    n = wt.shape[0] - row0 if n is None else n
    tm, tn = _tile(m, 1024), _tile(math.gcd(n, row0), 1024)
    j0 = row0 // tn
    return pl.pallas_call(
        _matmul_kernel,
        grid=(m // tm, n // tn),
        in_specs=[pl.BlockSpec((tm, k), lambda i, j: (i, 0)),
                  pl.BlockSpec((tn, k), lambda i, j: (j0 + j, 0))],
        out_specs=pl.BlockSpec((tm, tn), lambda i, j: (i, j)),
        out_shape=jax.ShapeDtypeStruct((m, n), out_dtype),
        compiler_params=_params("parallel", "arbitrary"),
        name=name,
    )(x, wt)


def _rotary_kernel(qk_ref, v_ref, cos_ref, sin_lo_ref, sin_hi_ref, o_ref, kmean_ref, vt_ref, *, n_heads, q_scale):
    cos, sin_lo, sin_hi = cos_ref[...], sin_lo_ref[...], sin_hi_ref[...]
    half = ROT_DIM // 2
    hd = ATTN_HEAD_DIM
    pad_rows = lax.broadcasted_iota(jnp.int32, (VT_ROWS - hd, MOBA_BLOCK), 0)
    ones_row = jnp.where(pad_rows == 0, 1.0, 0.0).astype(vt_ref.dtype)

    def rot(col):
        t = qk_ref[:, pl.ds(col, LANES)].astype(F32)
        return (t * cos + pltpu.roll(t, LANES - half, 1) * sin_lo + pltpu.roll(t, half, 1) * sin_hi)

    def head(h, carry):
        qcol = pl.multiple_of(h * LANES, LANES)
        kcol = pl.multiple_of((n_heads + h) * LANES, LANES)
        o_ref[:, pl.ds(qcol, LANES)] = (rot(qcol) * q_scale).astype(o_ref.dtype)
        k = rot(kcol)
        o_ref[:, pl.ds(kcol, LANES)] = k.astype(o_ref.dtype)
        kmean_ref[0, :, pl.ds(qcol, LANES)] = jnp.mean(k, axis=0, keepdims=True)
        vt_ref[0, h, 0:hd, :] = v_ref[:, pl.ds(qcol, LANES)].astype(F32).T.astype(vt_ref.dtype)
        vt_ref[0, h, hd:VT_ROWS, :] = ones_row
        return carry

    lax.fori_loop(0, n_heads, head, 0)


def _rotary(qkvg, bsz, seq, n_heads):
    t = qkvg.shape[0]
    width = n_heads * ATTN_HEAD_DIM
    nq = seq // MOBA_BLOCK
    half = ROT_DIM // 2
    inv_freq = ROPE_THETA ** (-jnp.arange(half, dtype=F32) * 2.0 / ROT_DIM)
    ang = jnp.arange(seq).astype(F32)[:, None] * inv_freq[None, :]
    cos, sin = jnp.cos(ang), jnp.sin(ang)
    zeros = lambda n: jnp.zeros((seq, n), F32)
    cos_t = jnp.concatenate([cos, cos, jnp.ones((seq, LANES - ROT_DIM), F32)], axis=1)
    sin_lo = jnp.concatenate([-sin, zeros(LANES - half)], axis=1)
    sin_hi = jnp.concatenate([zeros(half), sin, zeros(LANES - ROT_DIM)], axis=1)
    table_spec = pl.BlockSpec((MOBA_BLOCK, LANES), lambda r: (r % nq, 0))
    return pl.pallas_call(
        functools.partial(_rotary_kernel, n_heads=n_heads, q_scale=ATTN_HEAD_DIM ** -0.5 * LOG2_E),
        grid=(t // MOBA_BLOCK,),
        in_specs=[pl.BlockSpec((MOBA_BLOCK, 2 * width), lambda r: (r, 0)),
                  pl.BlockSpec((MOBA_BLOCK, width), lambda r: (r, 2)),
                  table_spec, table_spec, table_spec],
        out_specs=[pl.BlockSpec((MOBA_BLOCK, 2 * width), lambda r: (r, 0)),
                   pl.BlockSpec((1, 1, width), lambda r: (r, 0, 0)),
                   pl.BlockSpec((1, n_heads, VT_ROWS, MOBA_BLOCK), lambda r: (r // nq, 0, 0, r % nq))],
        out_shape=[jax.ShapeDtypeStruct((t, 2 * width), BF16),
                   jax.ShapeDtypeStruct((t // MOBA_BLOCK, 1, width), F32),
                   jax.ShapeDtypeStruct((bsz, n_heads, VT_ROWS, seq), BF16)],
        compiler_params=_params("parallel"),
        name="rotary_kmean",
    )(qkvg, qkvg, cos_t, sin_lo, sin_hi)


def _attn_kernel(q_ref, k_ref, vt_ref, g_ref, kmean_ref, o_ref, *scratch, nq, n_chains):
    sel_scr = scratch[0]
    s_even, s_odd = scratch[1:1 + n_chains], scratch[1 + n_chains:1 + 2 * n_chains]
    acc_scr = scratch[1 + 2 * n_chains:]
    i = pl.program_id(2)
    blk = MOBA_BLOCK
    hd = ATTN_HEAD_DIM
    neg_inf = -jnp.inf
    lead = (i + 1) % 2
    n_pairs = (i + 1 + lead) // 2
    heads = [slice(c * hd, (c + 1) * hd) for c in range(n_chains)]
    qs = [q_ref[:, hs] for hs in heads]

    def issue_scores(pos, dst):
        j = jnp.maximum(pos - lead, 0)
        rows = pl.ds(pl.multiple_of(j * blk, blk), blk)
        col_max = []
        for c in range(n_chains):
            s = lax.dot_general(k_ref[rows, heads[c]], qs[c], NT_DIMS, preferred_element_type=F32)
            dst[c][...] = s
            col_max.append(jnp.max(s, axis=0, keepdims=True))
        return col_max

    def consume(pos, src, col_max, m_run, own):
        j = jnp.maximum(pos - lead, 0)
        cols = pl.ds(pl.multiple_of(j * blk, blk), blk)
        sel_row = jnp.where(pos < lead, nq, j)
        m_out, staged = [], []
        for c in range(n_chains):
            s = src[c][...]
            if own:
                key_pos = lax.broadcasted_iota(jnp.int32, (blk, blk), 0)
                qry_pos = lax.broadcasted_iota(jnp.int32, (blk, blk), 1)
                s = jnp.where(key_pos <= qry_pos, s, neg_inf)
                chosen = None
                m_new = jnp.maximum(m_run[c], jnp.max(s, axis=0, keepdims=True))
                m_sub = m_new
                alpha = jnp.exp2(m_run[c] - m_new)
            else:
                chosen = sel_scr[c, pl.ds(sel_row, 1), :] > 0.0
                m_new = jnp.maximum(m_run[c], jnp.where(chosen, col_max[c], neg_inf))
                seen = m_new > neg_inf
                m_sub = jnp.where(seen, m_new, 0.0)
                alpha = jnp.where(seen, jnp.exp2(m_run[c] - m_new), 0.0)
            p = jnp.exp2(s - m_sub).astype(BF16)
            m_out.append(m_new)
            staged.append((alpha, p, chosen))
        for c in range(n_chains):
            alpha, p, chosen = staged[c]
            pv = jnp.dot(vt_ref[0, c, :, cols], p, preferred_element_type=F32)
            if chosen is not None:
                pv = jnp.where(chosen, pv, 0.0)
            acc_scr[c][...] = alpha * acc_scr[c][...] + pv
        return m_out

    gates = []
    for c in range(n_chains):
        gates.append(lax.dot_general(kmean_ref[0, :, heads[c]], qs[c].astype(F32), NT_DIMS,
                                     preferred_element_type=F32))
    cm0 = issue_scores(0, s_even)
    n_idx = lax.broadcasted_iota(jnp.int32, (nq, blk), 0)
    past = n_idx < i
    for c in range(n_chains):
        work = jnp.where(past, gates[c], neg_inf)
        chosen = jnp.zeros((nq, blk), jnp.bool_)
        for _ in range(MOBA_TOPK):
            top = jnp.max(work, axis=0, keepdims=True)
            first = jnp.min(jnp.where(work == top, n_idx, nq), axis=0, keepdims=True)
            hit = n_idx == first
            chosen = chosen | hit
            work = jnp.where(hit, neg_inf, work)
        sel_scr[c, 0:nq, :] = jnp.where(chosen & past, 1.0, 0.0)
        sel_scr[c, nq:nq + 8, :] = jnp.zeros((8, blk), F32)
        acc_scr[c][...] = jnp.zeros_like(acc_scr[c])

    m0 = [jnp.full((1, blk), neg_inf, F32) for _ in range(n_chains)]

    def pair(t, carry):
        m_run, cm_even = carry
        cm_odd = issue_scores(2 * t + 1, s_odd)
        m_run = consume(2 * t, s_even, cm_even, m_run, own=False)
        cm_even = issue_scores(2 * t + 2, s_even)
        m_run = consume(2 * t + 1, s_odd, cm_odd, m_run, own=False)
        return m_run, cm_even

    m_run, cm_even = lax.fori_loop(0, n_pairs - 1, pair, (m0, cm0))
    last = 2 * (n_pairs - 1)
    issue_scores(last + 1, s_odd)
    m_run = consume(last, s_even, cm_even, m_run, own=False)
    consume(last + 1, s_odd, None, m_run, own=True)

    for c in range(n_chains):
        acc = acc_scr[c][...]
        out = (acc[0:hd, :] / acc[hd:hd + 1, :]).T
        g = g_ref[:, heads[c]].astype(F32)
        o_ref[:, heads[c]] = (out * (g * jax.nn.sigmoid(g))).astype(o_ref.dtype)


ATTN_HEADS_PER_STEP = 8


def _attention(qk_rot, qkvg, kmean, vt, bsz, seq, n_heads):
    t = qk_rot.shape[0]
    nq = seq // MOBA_BLOCK
    width = n_heads * ATTN_HEAD_DIM
    hp = ATTN_HEADS_PER_STEP if n_heads % ATTN_HEADS_PER_STEP == 0 else 1
    n_hg, cw = n_heads // hp, hp * ATTN_HEAD_DIM
    kmean = kmean.reshape(bsz, nq, width)
    score_buf = pltpu.VMEM((MOBA_BLOCK, MOBA_BLOCK), F32)
    return pl.pallas_call(
        functools.partial(_attn_kernel, nq=nq, n_chains=hp),
        grid=(bsz, n_hg, nq),
        in_specs=[pl.BlockSpec((MOBA_BLOCK, cw), lambda b, h, i: (b * nq + i, h)),
                  pl.BlockSpec((seq, cw), lambda b, h, i: (b, n_hg + h)),
                  pl.BlockSpec((1, hp, VT_ROWS, seq), lambda b, h, i: (b, h, 0, 0)),
                  pl.BlockSpec((MOBA_BLOCK, cw), lambda b, h, i: (b * nq + i, 3 * n_hg + h)),
                  pl.BlockSpec((1, nq, cw), lambda b, h, i: (b, 0, h))],
        out_specs=pl.BlockSpec((MOBA_BLOCK, cw), lambda b, h, i: (b * nq + i, h)),
        out_shape=jax.ShapeDtypeStruct((t, width), BF16),
        scratch_shapes=([pltpu.VMEM((hp, nq + 8, MOBA_BLOCK), F32)]
                        + [score_buf] * (2 * hp)
                        + [pltpu.VMEM((VT_ROWS, MOBA_BLOCK), F32)] * hp),
        compiler_params=_params("parallel", "parallel", "arbitrary"),
        name="moba_attention",
    )(qk_rot, qk_rot, vt, qkvg, kmean)


def _xbc_proj_kernel(x_ref, w_ref, cw_ref, cb_ref, o_ref, raw_a, raw_b, carry_scr, *, n_tiles, n_steps, tiles_per_seq):
    t = pl.program_id(0)
    tm, tn = x_ref.shape[0], w_ref.shape[0]
    col_chunk, mm_rows, k_chunk = 256, 256, 512
    n_k = x_ref.shape[1] // k_chunk
    piece = mm_rows // n_k

    def step(src, dst):
        if src is not None:
            row_tile, col_tile = (t - 1) // n_tiles, (t - 1) % n_tiles
            prev = carry_scr[col_tile]
            src[0:CONV_CARRY_ROWS, :] = jnp.where(row_tile % tiles_per_seq == 0, jnp.zeros_like(prev), prev)
            carry_scr[col_tile] = src[tm:tm + CONV_CARRY_ROWS, :]
        for c in range(tn // col_chunk):
            cols = slice(c * col_chunk, (c + 1) * col_chunk)
            for rm in range(tm // mm_rows):
                rows = slice(rm * mm_rows, (rm + 1) * mm_rows)
                part = None
                for kc in range(n_k):
                    if dst is not None:
                        ks = slice(kc * k_chunk, (kc + 1) * k_chunk)
                        d = lax.dot_general(x_ref[rows, ks], w_ref[cols, ks], NT_DIMS, preferred_element_type=F32)
                        part = d if part is None else part + d
                    if src is not None:
                        r0 = rm * mm_rows + kc * piece
                        acc = jnp.broadcast_to(cb_ref[:, cols], (piece, col_chunk))
                        for tap in range(SSD_CONV):
                            off = r0 + CONV_CARRY_ROWS - (SSD_CONV - 1) + tap
                            acc = acc + src[off:off + piece, cols] * cw_ref[tap:tap + 1, cols]
                        o_ref[r0:r0 + piece, cols] = (acc * jax.nn.sigmoid(acc)).astype(o_ref.dtype)
                if dst is not None:
                    dst[CONV_CARRY_ROWS + rm * mm_rows:CONV_CARRY_ROWS + (rm + 1) * mm_rows, cols] = part

    @pl.when(t == 0)
    def _():
        step(None, raw_a)

    @pl.when((t % 2 == 1) & (t < n_steps))
    def _():
        step(raw_a, raw_b)

    @pl.when((t % 2 == 0) & (t > 0) & (t < n_steps))
    def _():
        step(raw_b, raw_a)

    @pl.when(t == n_steps)
    def _():
        step(raw_a if (n_steps - 1) % 2 == 0 else raw_b, None)


def _xbc_proj(u, wt, col0, conv_w, conv_b, seq):
    m, k = u.shape
    conv_dim = conv_w.shape[1]
    tm, tn = _tile(math.gcd(m, seq), 1024), _tile(math.gcd(conv_dim, col0), 1024)
    n_tiles, j0 = conv_dim // tn, col0 // tn
    n_steps = (m // tm) * n_tiles
    assert tm % 256 == 0 and tn % 256 == 0 and k % 512 == 0
    proj = lambda t: jnp.minimum(t, n_steps - 1)
    conv = lambda t: jnp.maximum(t - 1, 0)
    raw = pltpu.VMEM((CONV_CARRY_ROWS + tm, tn), F32)
    return pl.pallas_call(
        functools.partial(_xbc_proj_kernel, n_tiles=n_tiles, n_steps=n_steps, tiles_per_seq=seq // tm),
        grid=(n_steps + 1,),
        in_specs=[pl.BlockSpec((tm, k), lambda t: (proj(t) // n_tiles, 0)),
                  pl.BlockSpec((tn, k), lambda t: (j0 + proj(t) % n_tiles, 0)),
                  pl.BlockSpec((SSD_CONV, tn), lambda t: (0, conv(t) % n_tiles)),
                  pl.BlockSpec((1, tn), lambda t: (0, conv(t) % n_tiles))],
        out_specs=pl.BlockSpec((tm, tn), lambda t: (conv(t) // n_tiles, conv(t) % n_tiles)),
        out_shape=jax.ShapeDtypeStruct((m, conv_dim), BF16),
        scratch_shapes=[raw, raw, pltpu.VMEM((n_tiles, CONV_CARRY_ROWS, tn), F32)],
        compiler_params=_params("arbitrary"),
        name="in_proj_xbc_conv",
    )(u, wt, conv_w, conv_b.reshape(1, conv_dim))


def _split3_bf16(a):
    hi = a.astype(BF16)
    r = a - hi.astype(F32)
    mid = r.astype(BF16)
    lo = (r - mid.astype(F32)).astype(BF16)
    return hi, mid, lo


def _ssd_kernel(z_ref, xs_ref, bc_ref, dt_ref, dtb_ref, alog_ref, dskip_ref, normw_ref,
                o_ref, xe_scr, y_scr, state_scr, *, n_groups, heads_per_group):
    c = pl.program_id(1)
    L = SSD_CHUNK
    gw = heads_per_group * SSD_HEAD_DIM

    @pl.when(c == 0)
    def _():
        state_scr[...] = jnp.zeros_like(state_scr)

    xdt = dt_ref[...] + dtb_ref[...]
    dt = jnp.maximum(xdt, 0.0) + jnp.log1p(jnp.exp(-jnp.abs(xdt)))
    a = dt * (-jnp.exp(alog_ref[...])) * LOG2_E
    t_row = lax.broadcasted_iota(jnp.int32, (L, L), 0)
    t_col = lax.broadcasted_iota(jnp.int32, (L, L), 1)
    causal = t_col <= t_row
    tril = jnp.where(causal, 1.0, 0.0).astype(BF16)
    acum = sum(jnp.dot(tril, part, preferred_element_type=F32) for part in _split3_bf16(a))
    a_tot = acum[L - 1:L, :]
    log_dt = jnp.log2(dt)
    src_t = (acum - log_dt).T
    carry_exp = a_tot - acum + log_dt
    state_decay = jnp.exp2(a_tot)

    lane = lax.broadcasted_iota(jnp.int32, (L, LANES), 1)
    low_half = lane < SSD_HEAD_DIM
    n_bc = n_groups * SSD_STATE

    for g in range(n_groups):
        b_g = bc_ref[:, g * SSD_STATE:(g + 1) * SSD_STATE]
        c_g = bc_ref[:, n_bc + g * SSD_STATE:n_bc + (g + 1) * SSD_STATE]
        cb = lax.dot_general(c_g, b_g, NT_DIMS, preferred_element_type=F32)
        state_g = state_scr[g]
        y_off = jnp.dot(c_g, state_g.astype(BF16), preferred_element_type=F32)
        for pair in range(heads_per_group // 2):
            col0 = g * gw + pair * LANES
            x_pair = xs_ref[:, col0:col0 + LANES].astype(F32)
            x_lo = jnp.where(low_half, x_pair, 0.0).astype(BF16)
            x_hi = jnp.where(low_half, 0.0, x_pair).astype(BF16)
            h0 = g * heads_per_group + 2 * pair
            y_diag = jnp.zeros((L, LANES), F32)
            head_decay = []
            for h, x_half in ((h0, x_lo), (h0 + 1, x_hi)):
                acum_h = jnp.broadcast_to(acum[:, h:h + 1], (L, LANES))
                seg = jnp.concatenate([acum_h] * (L // LANES), axis=1) - src_t[h:h + 1, :]
                w = (cb * jnp.exp2(jnp.where(causal, seg, -jnp.inf))).astype(BF16)
                y_diag = y_diag + jnp.dot(w, x_half, preferred_element_type=F32)
                head_decay.append(jnp.exp2(acum_h))
            in_decay = jnp.where(low_half, head_decay[0], head_decay[1])
            y = (y_diag + in_decay * y_off[:, pair * LANES:(pair + 1) * LANES]
                 + dskip_ref[:, col0:col0 + LANES] * x_pair)
            zg = z_ref[:, col0:col0 + LANES].astype(F32)
            y_scr[:, col0:col0 + LANES] = y * (zg * jax.nn.sigmoid(zg))
            to_end = jnp.where(low_half, jnp.exp2(carry_exp[:, h0:h0 + 1]), jnp.exp2(carry_exp[:, h0 + 1:h0 + 2]))
            xe_scr[g, :, pair * LANES:(pair + 1) * LANES] = (x_pair * to_end).astype(xe_scr.dtype)
            sd = jnp.where(low_half[0:1, :], state_decay[:, h0:h0 + 1], state_decay[:, h0 + 1:h0 + 2])
            state_scr[g, :, pair * LANES:(pair + 1) * LANES] = state_g[:, pair * LANES:(pair + 1) * LANES] * sd
        state_scr[g] = state_scr[g] + lax.dot_general(b_g, xe_scr[g], TN_DIMS, preferred_element_type=F32)
        yg = y_scr[:, g * gw:(g + 1) * gw]
        ms = jnp.mean(yg * yg, axis=-1, keepdims=True)
        o_ref[:, g * gw:(g + 1) * gw] = (yg * lax.rsqrt(ms + NORM_EPS)
                                         * normw_ref[:, g * gw:(g + 1) * gw]).astype(o_ref.dtype)


def _ssd(z, xbc, dt_raw, dt_bias, a_log, d_skip, norm_w, bsz, seq, d_inner, n_heads, n_groups):
    t = z.shape[0]
    nc = seq // SSD_CHUNK
    bc_w = 2 * n_groups * SSD_STATE
    heads_per_group = n_heads // n_groups
    gw = heads_per_group * SSD_HEAD_DIM
    assert heads_per_group % 2 == 0 and d_inner % bc_w == 0 and n_heads <= LANES
    pad_heads = lambda v: jnp.pad(v.astype(F32), (0, LANES - n_heads)).reshape(1, LANES)
    row = lambda b, c: (b * nc + c, 0)
    const = lambda b, c: (0, 0)
    return pl.pallas_call(
        functools.partial(_ssd_kernel, n_groups=n_groups, heads_per_group=heads_per_group),
        grid=(bsz, nc),
        in_specs=[pl.BlockSpec((SSD_CHUNK, d_inner), row),
                  pl.BlockSpec((SSD_CHUNK, d_inner), row),
                  pl.BlockSpec((SSD_CHUNK, bc_w), lambda b, c: (b * nc + c, d_inner // bc_w)),
                  pl.BlockSpec((SSD_CHUNK, LANES), row),
                  pl.BlockSpec((1, LANES), const),
                  pl.BlockSpec((1, LANES), const),
                  pl.BlockSpec((1, d_inner), const),
                  pl.BlockSpec((1, d_inner), const)],
        out_specs=pl.BlockSpec((SSD_CHUNK, d_inner), row),
        out_shape=jax.ShapeDtypeStruct((t, d_inner), BF16),
        scratch_shapes=[pltpu.VMEM((n_groups, SSD_CHUNK, gw), BF16),
                        pltpu.VMEM((SSD_CHUNK, d_inner), F32),
                        pltpu.VMEM((n_groups, SSD_STATE, gw), F32)],
        compiler_params=_params("parallel", "arbitrary"),
        name="ssd_scan",
    )(z, xbc, xbc, dt_raw, pad_heads(dt_bias), pad_heads(a_log),
      jnp.repeat(d_skip.astype(F32), SSD_HEAD_DIM).reshape(1, d_inner), norm_w.reshape(1, d_inner))


def _merge_kernel(oa_ref, wa_ref, ys_ref, ws_ref, ga_ref, gs_ref, ba_ref, bs_ref, o_ref):
    y_attn = jnp.dot(oa_ref[...], wa_ref[...], preferred_element_type=F32)
    y_ssd = jnp.dot(ys_ref[...], ws_ref[...], preferred_element_type=F32)
    gate_a = jax.nn.sigmoid(ga_ref[...].astype(F32) + ba_ref[...])
    gate_s = jax.nn.sigmoid(gs_ref[...].astype(F32) + bs_ref[...])
    o_ref[...] = (gate_a * y_attn + gate_s * y_ssd).astype(o_ref.dtype)


def _merge(o_attn, w_attn_out, y_ssd, w_ssd_out, g_merge, gate_bias):
    t, wa = o_attn.shape
    ws = y_ssd.shape[1]
    d = w_attn_out.shape[1]
    tm, tn = _tile(t, 1024), _tile(d, 512)
    nj = d // tn
    bias = gate_bias.astype(F32).reshape(1, 2 * d)
    return pl.pallas_call(
        _merge_kernel,
        grid=(t // tm, nj),
        in_specs=[pl.BlockSpec((tm, wa), lambda i, j: (i, 0)),
                  pl.BlockSpec((wa, tn), lambda i, j: (0, j)),
                  pl.BlockSpec((tm, ws), lambda i, j: (i, 0)),
                  pl.BlockSpec((ws, tn), lambda i, j: (0, j)),
                  pl.BlockSpec((tm, tn), lambda i, j: (i, j)),
                  pl.BlockSpec((tm, tn), lambda i, j: (i, nj + j)),
                  pl.BlockSpec((1, tn), lambda i, j: (0, j)),
                  pl.BlockSpec((1, tn), lambda i, j: (0, nj + j))],
        out_specs=pl.BlockSpec((tm, tn), lambda i, j: (i, j)),
        out_shape=jax.ShapeDtypeStruct((t, d), BF16),
        compiler_params=_params("parallel", "arbitrary"),
        name="branch_merge",
    )(o_attn, w_attn_out, y_ssd, w_ssd_out, g_merge, g_merge, bias, bias)


def _out_kernel(m_ref, w_ref, x_ref, fw_ref, o_ref, ssq_scr, *, tn, d_model, final_norm):
    j = pl.program_id(1)

    @pl.when(j == 0)
    def _():
        ssq_scr[...] = jnp.zeros_like(ssq_scr)

    h = x_ref[...] + jnp.dot(m_ref[...], w_ref[...], preferred_element_type=F32)
    o_ref[:, pl.ds(pl.multiple_of(j * tn, tn), tn)] = h
    ssq_scr[...] += jnp.sum(h * h, axis=-1, keepdims=True)

    if final_norm:
        @pl.when(j == pl.num_programs(1) - 1)
        def _():
            inv = lax.rsqrt(ssq_scr[...] * (1.0 / d_model) + NORM_EPS)
            for c in range(d_model // tn):
                cols = slice(c * tn, (c + 1) * tn)
                o_ref[:, cols] = o_ref[:, cols] * inv * fw_ref[:, cols]


def _out_proj(merged, w_out, x, final_norm_w, final_norm):
    t, d = x.shape
    tm, tn = _tile(t, 512), _tile(d, 1024)
    return pl.pallas_call(
        functools.partial(_out_kernel, tn=tn, d_model=d, final_norm=final_norm),
        grid=(t // tm, d // tn),
        in_specs=[pl.BlockSpec((tm, d), lambda i, j: (i, 0)),
                  pl.BlockSpec((d, tn), lambda i, j: (0, j)),
                  pl.BlockSpec((tm, tn), lambda i, j: (i, j)),
                  pl.BlockSpec((1, d), lambda i, j: (0, 0))],
        out_specs=pl.BlockSpec((tm, d), lambda i, j: (i, 0)),
        out_shape=jax.ShapeDtypeStruct((t, d), F32),
        scratch_shapes=[pltpu.VMEM((tm, 1), F32)],
        compiler_params=_params("parallel", "arbitrary"),
        name="out_proj_norm",
    )(merged, w_out, x, final_norm_w.astype(F32).reshape(1, d))


def kernel(x, norm_w, w_in, conv_w, conv_b, dt_bias, a_log, d_skip, ssd_norm_w, w_attn_out, w_ssd_out,
           gate_bias, w_out, final_norm_w):
    bsz, seq, d_model = x.shape
    depth = norm_w.shape[0]
    attn_w = w_attn_out.shape[1]
    d_inner = w_ssd_out.shape[1]
    n_ssd_heads = a_log.shape[1]
    conv_dim = conv_w.shape[2]
    n_groups = (conv_dim - d_inner) // (2 * SSD_STATE)
    n_attn_heads = attn_w // ATTN_HEAD_DIM
    assert seq % MOBA_BLOCK == 0 and seq % SSD_CHUNK == 0
    c_zx, c_dt = 4 * attn_w, 4 * attn_w + d_inner + conv_dim
    c_gm = c_dt + n_ssd_heads

    h = x.reshape(bsz * seq, d_model)
    for l in range(depth):
        wt = jnp.swapaxes(w_in[l], 0, 1)
        w_main = _cast_rows(wt, 0, c_dt, "cast_w_main")
        w_dt = _cast_rows(wt, c_dt, LANES, "cast_w_dt")
        w_gate = _cast_rows_unaligned(wt, c_gm, wt.shape[0] - c_gm, "cast_w_gate")
        u = _rmsnorm(h, norm_w[l], BF16)
        qkvg = _matmul(u, w_main, BF16, "in_proj_attn", 0, c_zx)
        z = _matmul(u, w_main, BF16, "in_proj_z", c_zx, d_inner)
        xbc = _xbc_proj(u, w_main, c_zx + d_inner, conv_w[l], conv_b[l], seq)
        dt_raw = _matmul(u, w_dt, F32, "in_proj_dt")
        g_merge = _matmul(u, w_gate, BF16, "in_proj_gate")

        qk_rot, kmean, vt = _rotary(qkvg, bsz, seq, n_attn_heads)
        o_attn = _attention(qk_rot, qkvg, kmean, vt, bsz, seq, n_attn_heads)
        y_ssd = _ssd(z, xbc, dt_raw, dt_bias[l], a_log[l], d_skip[l], ssd_norm_w[l],
                     bsz, seq, d_inner, n_ssd_heads, n_groups)
        merged = _merge(o_attn, w_attn_out[l].astype(BF16), y_ssd, w_ssd_out[l].astype(BF16), g_merge,
                        gate_bias[l])
        h = _out_proj(merged, w_out[l].astype(BF16), h, final_norm_w, final_norm=(l == depth - 1))
    return h.reshape(bsz, seq, d_model)
```
